```python
import math
import jax
import jax.numpy as jnp
from jax import lax
import numpy as np

D_MODEL = 1024
BATCH = 16
SEQ = 2048
DEPTH = 2

CHUNK = 64
Q_BLOCK = 128
N_BRANCH = 3
BRANCH_W = D_MODEL
EPS = 1e-6

FOX_HD = 64
FOX_HEADS = BRANCH_W // FOX_HD

SSM_HD = 64
SSM_HEADS = BRANCH_W // SSM_HD
SSM_W = SSM_HEADS * SSM_HD
SSM_GROUPS = 4
SSM_STATE = 128
SSM_CONV_K = 4
SSM_CONV_DIM = SSM_W + 2 * SSM_GROUPS * SSM_STATE

DIFF_HD = 64
DIFF_HEADS = BRANCH_W // (2 * DIFF_HD)
DIFF_QK_W = DIFF_HEADS * 2 * DIFF_HD
DIFF_V_W = DIFF_HEADS * 2 * DIFF_HD
ROPE_THETA = 500000.0
ROT_DIM = DIFF_HD // 4

SPLIT_SIZES = (
    BRANCH_W, BRANCH_W, BRANCH_W, FOX_HEADS, BRANCH_W,
    SSM_W, SSM_CONV_DIM, SSM_HEADS,
    DIFF_QK_W, DIFF_QK_W, DIFF_V_W, DIFF_V_W,
    N_BRANCH * D_MODEL,
)
N_IN = sum(SPLIT_SIZES)

kernel_name = 'hybrid_fox_ssd_diffattn_streaming'


def rmsnorm(x, w):
    x32 = x.astype(jnp.float32)
    y = x32 * lax.rsqrt(jnp.mean(x32 * x32, axis=-1, keepdims=True) + EPS)
    return (y * w.astype(jnp.float32)).astype(x.dtype)


def partial_rope(x, cos, sin):
    half = ROT_DIM // 2
    x1 = x[..., :half]
    x2 = x[..., half:ROT_DIM]
    rest = x[..., ROT_DIM:]
    rot = jnp.concatenate([x1 * cos - x2 * sin, x2 * cos + x1 * sin], axis=-1)
    return jnp.concatenate([rot.astype(x.dtype), rest], axis=-1)


def causal_depthwise_conv(u, w, b):
    y = lax.conv_general_dilated(
        u, w[:, None, :].astype(u.dtype), window_strides=(1,),
        padding=[(SSM_CONV_K - 1, 0)], dimension_numbers=('NWC', 'WIO', 'NWC'),
        feature_group_count=u.shape[-1])
    return y + b


def forgetting_attention(q, k, v, log_f):
    L = q.shape[2]
    F = jnp.cumsum(log_f, axis=-1)
    scale = q.shape[-1] ** -0.5
    outs = []
    for i in range(L // Q_BLOCK):
        q0, q1 = i * Q_BLOCK, (i + 1) * Q_BLOCK
        s = jnp.einsum('bhqd,bhkd->bhqk', q[:, :, q0:q1], k[:, :, :q1]).astype(jnp.float32) * scale
        s = s + F[:, :, q0:q1, None] - F[:, :, None, :q1]
        tq = jnp.arange(q0, q1)[:, None]
        tk = jnp.arange(q1)[None, :]
        s = jnp.where(tk <= tq, s, -jnp.inf)
        p = jax.nn.softmax(s, axis=-1).astype(v.dtype)
        outs.append(jnp.einsum('bhqk,bhkd->bhqd', p, v[:, :, :q1]))
    return jnp.concatenate(outs, axis=2)


def differential_attention(q, k, v, lam):
    L = q.shape[3]
    scale = q.shape[-1] ** -0.5
    outs = []
    for i in range(L // Q_BLOCK):
        q0, q1 = i * Q_BLOCK, (i + 1) * Q_BLOCK
        s = jnp.einsum('bhcqd,bhckd->bhcqk', q[:, :, :, q0:q1], k[:, :, :, :q1]).astype(jnp.float32) * scale
        cq = (jnp.arange(q0, q1) // CHUNK)[:, None]
        ck = (jnp.arange(q1) // CHUNK)[None, :]
        s = jnp.where(ck <= cq, s, -jnp.inf)
        p = jax.nn.softmax(s, axis=-1)
        a = (p[:, :, 0] - lam * p[:, :, 1]).astype(v.dtype)
        outs.append(jnp.einsum('bhqk,bhkd->bhqd', a, v[:, :, :q1]))
    return jnp.concatenate(outs, axis=2)


def ssd_scan(x, dt, a, b, c):
    bsz, L, H, P = x.shape
    G, N = b.shape[-2], b.shape[-1]
    nc = L // CHUNK
    hg = H // G
    x = x.reshape(bsz, nc, CHUNK, G, hg, P)
    dt = dt.reshape(bsz, nc, CHUNK, G, hg)
    b = b.reshape(bsz, nc, CHUNK, G, N)
    c = c.reshape(bsz, nc, CHUNK, G, N)
    xdt = x * dt[..., None].astype(x.dtype)
    a_cs = jnp.cumsum(dt * a.reshape(G, hg), axis=2)
    seg = a_cs[:, :, :, None] - a_cs[:, :, None, :]
    tril = jnp.tril(jnp.ones((CHUNK, CHUNK), dtype=bool))[:, :, None, None]
    decay = jnp.exp(jnp.where(tril, seg, -jnp.inf))
    cb = jnp.einsum('bclgn,bcsgn->bclsg', c, b)
    y_diag = jnp.einsum('bclsg,bclsgh,bcsghp->bclghp', cb, decay, xdt)
    decay_states = jnp.exp(a_cs[:, :, -1:] - a_cs)
    states = jnp.einsum('bcsgn,bcsgh,bcsghp->bcghpn', b, decay_states, xdt)
    chunk_decay = jnp.exp(a_cs[:, :, -1])

    def step(h, inp):
        st, dec = inp
        h_new = h * dec[..., None, None] + st
        return h_new, h

    init = jnp.zeros((bsz, G, hg, P, N), dtype=states.dtype)
    _, prev = lax.scan(step, init, (states.swapaxes(0, 1), chunk_decay.swapaxes(0, 1)))
    prev = prev.swapaxes(0, 1)
    y_off = jnp.einsum('bclgn,bcghpn,bclgh->bclghp', c, prev, jnp.exp(a_cs))
    return (y_diag + y_off).reshape(bsz, L, H, P)


def hybrid_layer(x, layer, norm_w, w_in, b_forget, conv_w, conv_b, dt_bias, a_log,
                 d_skip, ssm_norm_w, diff_lambda, subln_w, w_branch, w_out, cos, sin):
    bsz, L, _ = x.shape
    h = rmsnorm(x, norm_w)
    proj = jnp.einsum('bld,de->ble', h, w_in)
    split_points = np.cumsum(SPLIT_SIZES)[:-1].tolist()
    (fq, fk, fv, ff, fg, sz, sxbc, sdt, dq, dk, dv, dg, mg) = jnp.split(proj, split_points, axis=-1)

    def heads(t, n, d):
        return t.reshape(bsz, L, n, d).transpose(0, 2, 1, 3)
    log_f = jax.nn.log_sigmoid((ff + b_forget).astype(jnp.float32)).transpose(0, 2, 1)
    o_a = forgetting_attention(heads(fq, FOX_HEADS, FOX_HD), heads(fk, FOX_HEADS, FOX_HD),
                               heads(fv, FOX_HEADS, FOX_HD), log_f)
    y_a = o_a.transpose(0, 2, 1, 3).reshape(bsz, L, BRANCH_W) * jax.nn.silu(fg)

    xbc = jax.nn.silu(causal_depthwise_conv(sxbc, conv_w, conv_b))
    gn = SSM_GROUPS * SSM_STATE
    xs, bs, cs = jnp.split(xbc, [SSM_W, SSM_W + gn], axis=-1)
    dt = jax.nn.softplus((sdt + dt_bias).astype(jnp.float32))
    a = -jnp.exp(a_log.astype(jnp.float32))
    xh = xs.reshape(bsz, L, SSM_HEADS, SSM_HD)
    y = ssd_scan(xh, dt, a, bs.reshape(bsz, L, SSM_GROUPS, SSM_STATE),
                 cs.reshape(bsz, L, SSM_GROUPS, SSM_STATE))
    y = (y + xh * d_skip[:, None]).reshape(bsz, L, SSM_W)
    yg = (y * jax.nn.silu(sz)).reshape(bsz, L, SSM_GROUPS, SSM_W // SSM_GROUPS)
    y_b = rmsnorm(yg, jnp.ones((SSM_W // SSM_GROUPS,), jnp.float32)).reshape(bsz, L, SSM_W) * ssm_norm_w

    q = partial_rope(dq.reshape(bsz, L, DIFF_HEADS, 2, DIFF_HD), cos, sin).transpose(0, 2, 3, 1, 4)
    k = partial_rope(dk.reshape(bsz, L, DIFF_HEADS, 2, DIFF_HD), cos, sin).transpose(0, 2, 3, 1, 4)
    v = heads(dv, DIFF_HEADS, 2 * DIFF_HD)
    lam_init = 0.8 - 0.6 * math.exp(-0.3 * layer)
    lp = diff_lambda.astype(jnp.float32)
    lam = jnp.exp(jnp.sum(lp[0] * lp[1])) - jnp.exp(jnp.sum(lp[2] * lp[3])) + lam_init
    o_c = differential_attention(q, k, v, lam)
    o_c = rmsnorm(o_c, subln_w) * (1.0 - lam_init)
    y_c = o_c.transpose(0, 2, 1, 3).reshape(bsz, L, DIFF_V_W) * jax.nn.silu(dg)

    gates = jax.nn.sigmoid(mg).reshape(bsz, L, N_BRANCH, D_MODEL)
    branches = jnp.stack([y_a, y_b, y_c], axis=2)
    proj_br = jnp.einsum('blnw,nwd->blnd', branches, w_branch)
    merged = jnp.sum(gates * proj_br, axis=2)
    return x + jnp.einsum('bld,de->ble', merged, w_out)


def setup_inputs(seed: int = 0) -> dict:
    key = jax.random.key(seed)
    ks = jax.random.split(key, 16)
    f32 = jnp.float32
    x = jax.random.normal(ks[0], (BATCH, SEQ, D_MODEL), f32)
    norm_w = 1.0 + 0.02 * jax.random.normal(ks[1], (DEPTH, D_MODEL), f32)
    w_in = jax.random.normal(ks[2], (DEPTH, D_MODEL, N_IN), f32) * D_MODEL ** -0.5
    b_forget = jax.random.uniform(ks[3], (DEPTH, FOX_HEADS), f32, minval=1.0, maxval=4.0)
    conv_w = jax.random.normal(ks[4], (DEPTH, SSM_CONV_K, SSM_CONV_DIM), f32) * SSM_CONV_K ** -0.5
    conv_b = 0.01 * jax.random.normal(ks[5], (DEPTH, SSM_CONV_DIM), f32)
    dt0 = jnp.exp(jax.random.uniform(ks[6], (DEPTH, SSM_HEADS), f32,
                                     minval=math.log(1e-3), maxval=math.log(1e-1)))
    dt_bias = dt0 + jnp.log(-jnp.expm1(-dt0))
    a_log = jnp.log(jax.random.uniform(ks[7], (DEPTH, SSM_HEADS), f32, minval=1.0, maxval=16.0))
    d_skip = 1.0 + 0.02 * jax.random.normal(ks[8], (DEPTH, SSM_HEADS), f32)
    ssm_norm_w = 1.0 + 0.02 * jax.random.normal(ks[9], (DEPTH, SSM_W), f32)
    diff_lambda = 0.1 * jax.random.normal(ks[10], (DEPTH, 4, DIFF_HD), f32)
    subln_w = 1.0 + 0.02 * jax.random.normal(ks[11], (DEPTH, 2 * DIFF_HD), f32)
    w_branch = jax.random.normal(ks[12], (DEPTH, N_BRANCH, BRANCH_W, D_MODEL), f32) * BRANCH_W ** -0.5
    w_out = jax.random.normal(ks[13], (DEPTH, D_MODEL, D_MODEL), f32) * D_MODEL ** -0.5
    final_norm_w = 1.0 + 0.02 * jax.random.normal(ks[14], (D_MODEL,), f32)
    return {'x': x, 'norm_w': norm_w, 'w_in': w_in, 'b_forget': b_forget, 'conv_w': conv_w,
            'conv_b': conv_b, 'dt_bias': dt_bias, 'a_log': a_log, 'd_skip': d_skip,
            'ssm_norm_w': ssm_norm_w, 'diff_lambda': diff_lambda, 'subln_w': subln_w,
            'w_branch': w_branch, 'w_out': w_out, 'final_norm_w': final_norm_w}


def reference(x, norm_w, w_in, b_forget, conv_w, conv_b, dt_bias, a_log, d_skip,
              ssm_norm_w, diff_lambda, subln_w, w_branch, w_out, final_norm_w):
    L = x.shape[1]
    pos = jnp.arange(L, dtype=jnp.float32)
    inv_freq = ROPE_THETA ** (-jnp.arange(0, ROT_DIM, 2, dtype=jnp.float32) / ROT_DIM)
    ang = pos[:, None] * inv_freq[None, :]
    cos = jnp.cos(ang)[:, None, None, :]
    sin = jnp.sin(ang)[:, None, None, :]
    for layer in range(DEPTH):
        x = hybrid_layer(x, layer, norm_w[layer], w_in[layer], b_forget[layer], conv_w[layer],
                         conv_b[layer], dt_bias[layer], a_log[layer], d_skip[layer],
                         ssm_norm_w[layer], diff_lambda[layer], subln_w[layer],
                         w_branch[layer], w_out[layer], cos, sin)
    return rmsnorm(x, final_norm_w)
```

```python
import functools
import math

import numpy as np
import jax
import jax.numpy as jnp
from jax import lax
from jax.experimental import pallas as pl
from jax.experimental.pallas import tpu as pltpu

F32 = jnp.float32
BF16 = jnp.bfloat16

D_MODEL = 1024
EPS = 1e-6
HEAD_DIM = 64
LANES = 128
N_SMALL = 16
SSM_GROUPS = 4
SSM_STATE = 128
SSM_GROUP_W = 256
CONV_K = 4
CONV_HALO = 8
DIFF_HEADS = 8
ROT_HALF = 8
ROPE_THETA = 500000.0
MASK_CHUNK = 64

C_FQ, C_FK, C_FV, C_FG = 0, 1024, 2048, 3072
C_XBC, C_SZ = 4096, 6144
C_DQ, C_DK, C_DV, C_DG = 7168, 8192, 9216, 10240
C_MG = 11264
N_MAIN = 14336

ATT_TILE = 128
SSD_CHUNK = 128
PROJ_TM, PROJ_TN = 1024, 1024
MERGE_TM = 512
VMEM_LIMIT = 56 * 1024 * 1024


def _sigmoid(x):
    return 1.0 / (1.0 + jnp.exp(-x))


def _softplus(x):
    return jnp.maximum(x, 0.0) + jnp.log1p(jnp.exp(-jnp.abs(x)))


def _split3(x):
    hi = x.astype(BF16)
    r1 = x - hi.astype(F32)
    mid = r1.astype(BF16)
    lo = (r1 - mid.astype(F32)).astype(BF16)
    return hi, mid, lo


def _dot(a, b):
    return jnp.dot(a, b, preferred_element_type=F32)


def _dot_nt(a, b):
    return lax.dot_general(a, b, (((1,), (1,)), ((), ())), preferred_element_type=F32)


def _exact_dot_r(x, m01):
    hi, mid, lo = _split3(x)
    return _dot(hi, m01) + _dot(mid, m01) + _dot(lo, m01)


def _exact_dot_l(m01, x):
    hi, mid, lo = _split3(x)
    return _dot(m01, hi) + _dot(m01, mid) + _dot(m01, lo)


def _tril_bf16(n):
    r = lax.broadcasted_iota(jnp.int32, (n, n), 0)
    c = lax.broadcasted_iota(jnp.int32, (n, n), 1)
    return jnp.where(r >= c, 1.0, 0.0).astype(BF16)


def _inproj_kernel(x_ref, nw_ref, w_ref, ws_ref, p_ref, s_ref, h_ref):
    @pl.when(pl.program_id(1) == 0)
    def _():
        x = x_ref[...]
        ms = jnp.mean(x * x, axis=-1, keepdims=True)
        h = (x * lax.rsqrt(ms + EPS) * nw_ref[...]).astype(BF16)
        h_ref[...] = h
        s_ref[...] = _dot(h, ws_ref[...])

    p_ref[...] = _dot(h_ref[...], w_ref[...]).astype(BF16)


def _inproj(x2, norm_w, w_main, w_small):
    t = x2.shape[0]
    tm, tn = min(PROJ_TM, t), PROJ_TN
    return pl.pallas_call(
        _inproj_kernel,
        out_shape=(jax.ShapeDtypeStruct((t, N_MAIN), BF16),
                   jax.ShapeDtypeStruct((t, LANES), F32)),
        grid=(t // tm, N_MAIN // tn),
        in_specs=[
            pl.BlockSpec((tm, D_MODEL), lambda i, j: (i, 0)),
            pl.BlockSpec((1, D_MODEL), lambda i, j: (0, 0)),
            pl.BlockSpec((D_MODEL, tn), lambda i, j: (0, j)),
            pl.BlockSpec((D_MODEL, LANES), lambda i, j: (0, 0)),
        ],
        out_specs=(pl.BlockSpec((tm, tn), lambda i, j: (i, j)),
                   pl.BlockSpec((tm, LANES), lambda i, j: (i, 0))),
        scratch_shapes=[pltpu.VMEM((tm, D_MODEL), BF16)],
        compiler_params=pltpu.CompilerParams(
            dimension_semantics=("arbitrary", "arbitrary"), vmem_limit_bytes=VMEM_LIMIT),
        name="inproj",
    )(x2, norm_w, w_main, w_small)


def _fprep_kernel(s_ref, bias_ref, ft_ref, f_scr, *, seq):
    tril = _tril_bf16(LANES)
    carry = jnp.zeros((1, LANES), F32)
    for blk in range(seq // LANES):
        x = s_ref[blk * LANES:(blk + 1) * LANES, :] + bias_ref[...]
        log_f = jnp.minimum(x, 0.0) - jnp.log1p(jnp.exp(-jnp.abs(x)))
        c = _exact_dot_l(tril, log_f) + carry
        f_scr[blk * LANES:(blk + 1) * LANES, :] = c
        carry = c[LANES - 1:LANES, :]
    ft_ref[...] = f_scr[...].T[0:N_SMALL, :]


def _fprep(small, bias_pad, bsz, seq):
    return pl.pallas_call(
        functools.partial(_fprep_kernel, seq=seq),
        out_shape=jax.ShapeDtypeStruct((bsz, N_SMALL, seq), F32),
        grid=(bsz,),
        in_specs=[pl.BlockSpec((seq, LANES), lambda b: (b, 0)),
                  pl.BlockSpec((1, LANES), lambda b: (0, 0))],
        out_specs=pl.BlockSpec((None, N_SMALL, seq), lambda b: (b, 0, 0)),
        scratch_shapes=[pltpu.VMEM((seq, LANES), F32)],
        compiler_params=pltpu.CompilerParams(
            dimension_semantics=("arbitrary",), vmem_limit_bytes=VMEM_LIMIT),
        name="fprep",
    )(small, bias_pad)


def _attend(qm, k_ref, v_ref, i, tile, diag_mask, bias_fn):
    def scores(j):
        c = pl.multiple_of(j * tile, tile)
        s = _dot_nt(qm, k_ref[pl.ds(c, tile), :])
        if bias_fn is not None:
            s = s - bias_fn(c)
        return s, c

    s, c = scores(i)
    s = jnp.where(diag_mask, s, -jnp.inf)
    m = jnp.max(s, axis=1, keepdims=True)
    p = jnp.exp(s - m)
    l = jnp.sum(p, axis=1, keepdims=True)
    acc = _dot(p.astype(BF16), v_ref[pl.ds(c, tile), :])

    def body(j, carry):
        m, l, acc = carry
        s, c = scores(j)
        m_new = jnp.maximum(m, jnp.max(s, axis=1, keepdims=True))
        alpha = jnp.exp(m - m_new)
        p = jnp.exp(s - m_new)
        l = alpha * l + jnp.sum(p, axis=1, keepdims=True)
        acc = alpha * acc + _dot(p.astype(BF16), v_ref[pl.ds(c, tile), :])
        return m_new, l, acc

    _, l, acc = lax.fori_loop(0, i, body, (m, l, acc))
    return acc, l


def _fox_kernel(q_ref, k_ref, v_ref, g_ref, ft_ref, o_ref, *, seq):
    tile = ATT_TILE
    lane = lax.broadcasted_iota(jnp.int32, (1, LANES), 1)
    row = lax.broadcasted_iota(jnp.int32, (tile, tile), 0)
    col = lax.broadcasted_iota(jnp.int32, (tile, tile), 1)
    causal = col <= row
    first = lane < HEAD_DIM

    def q_body(i, _):
        r = pl.multiple_of(i * tile, tile)
        q = q_ref[pl.ds(r, tile), :]
        outs = []
        for hh in range(2):
            qm = jnp.where(first if hh == 0 else jnp.logical_not(first), q, jnp.zeros_like(q))
            bias_fn = lambda c, hh=hh: ft_ref[hh:hh + 1, pl.ds(c, tile)]
            acc, l = _attend(qm, k_ref, v_ref, i, tile, causal, bias_fn)
            outs.append(acc / l)
        o = jnp.where(first, outs[0], outs[1])
        g = g_ref[pl.ds(r, tile), :].astype(F32)
        o_ref[pl.ds(r, tile), :] = (o * (g * _sigmoid(g))).astype(BF16)
        return 0

    lax.fori_loop(0, seq // tile, q_body, 0)


def _fox(p, ft4, bsz, seq):
    blk = lambda off: pl.BlockSpec((seq, LANES), lambda b, h, off=off: (b, off // LANES + h))
    return pl.pallas_call(
        functools.partial(_fox_kernel, seq=seq),
        out_shape=jax.ShapeDtypeStruct((bsz * seq, D_MODEL), BF16),
        grid=(bsz, D_MODEL // LANES),
        in_specs=[blk(C_FQ), blk(C_FK), blk(C_FV), blk(C_FG),
                  pl.BlockSpec((None, None, 2, seq), lambda b, h: (b, h, 0, 0))],
        out_specs=pl.BlockSpec((seq, LANES), lambda b, h: (b, h)),
        compiler_params=pltpu.CompilerParams(
            dimension_semantics=("arbitrary", "arbitrary"), vmem_limit_bytes=VMEM_LIMIT),
        name="fox",
    )(p, p, p, p, ft4)


def _diff_kernel(q_ref, k_ref, v_ref, g_ref, cos_ref, s1_ref, s2_ref, lam_ref, sub_ref,
                 o_ref, qr_scr, kr_scr, *, seq, lam_init):
    tile = ATT_TILE
    rows = 256 if seq % 256 == 0 else seq

    def rope(src, dst):
        for r0 in range(0, seq, rows):
            x = src[r0:r0 + rows, :].astype(F32)
            xr = (x * cos_ref[r0:r0 + rows, :]
                  + pltpu.roll(x, LANES - ROT_HALF, 1) * s1_ref[r0:r0 + rows, :]
                  + pltpu.roll(x, ROT_HALF, 1) * s2_ref[r0:r0 + rows, :])
            dst[r0:r0 + rows, :] = xr.astype(BF16)

    rope(q_ref, qr_scr)
    rope(k_ref, kr_scr)

    lp = lam_ref[...]
    lam = (jnp.exp(jnp.sum(lp[0:1] * lp[1:2], axis=1, keepdims=True))
           - jnp.exp(jnp.sum(lp[2:3] * lp[3:4], axis=1, keepdims=True)) + lam_init)

    lane = lax.broadcasted_iota(jnp.int32, (1, LANES), 1)
    first = lane < HEAD_DIM
    row = lax.broadcasted_iota(jnp.int32, (tile, tile), 0)
    col = lax.broadcasted_iota(jnp.int32, (tile, tile), 1)
    chunk_causal = (col // MASK_CHUNK) <= (row // MASK_CHUNK)

    def q_body(i, _):
        r = pl.multiple_of(i * tile, tile)
        q = qr_scr[pl.ds(r, tile), :]
        outs = []
        for cc in range(2):
            qm = jnp.where(first if cc == 0 else jnp.logical_not(first), q, jnp.zeros_like(q))
            acc, l = _attend(qm, kr_scr, v_ref, i, tile, chunk_causal, None)
            outs.append(acc / l)
        o = outs[0] - lam * outs[1]
        ms = jnp.mean(o * o, axis=-1, keepdims=True)
        o = o * lax.rsqrt(ms + EPS) * sub_ref[...] * (1.0 - lam_init)
        g = g_ref[pl.ds(r, tile), :].astype(F32)
        o_ref[pl.ds(r, tile), :] = (o * (g * _sigmoid(g))).astype(BF16)
        return 0

    lax.fori_loop(0, seq // tile, q_body, 0)


def _diff(p, cos_t, s1_t, s2_t, lam_p, sub_w, bsz, seq, lam_init):
    blk = lambda off: pl.BlockSpec((seq, LANES), lambda b, h, off=off: (b, off // LANES + h))
    const = lambda shape: pl.BlockSpec(shape, lambda b, h: (0, 0))
    return pl.pallas_call(
        functools.partial(_diff_kernel, seq=seq, lam_init=lam_init),
        out_shape=jax.ShapeDtypeStruct((bsz * seq, D_MODEL), BF16),
        grid=(bsz, DIFF_HEADS),
        in_specs=[blk(C_DQ), blk(C_DK), blk(C_DV), blk(C_DG),
                  const((seq, LANES)), const((seq, LANES)), const((seq, LANES)),
                  const((4, HEAD_DIM)), const((1, LANES))],
        out_specs=pl.BlockSpec((seq, LANES), lambda b, h: (b, h)),
        scratch_shapes=[pltpu.VMEM((seq, LANES), BF16), pltpu.VMEM((seq, LANES), BF16)],
        compiler_params=pltpu.CompilerParams(
            dimension_semantics=("arbitrary", "arbitrary"), vmem_limit_bytes=VMEM_LIMIT),
        name="diffattn",
    )(p, p, p, p, cos_t, s1_t, s2_t, lam_p, sub_w)


def _ssd_kernel(xbc_ref, z_ref, sm_ref, cw_ref, cb_ref, dtb_ref, alog_ref, dskip_ref, nw_ref,
                e_ref, o_ref, u_scr, st_scr, *, q):
    @pl.when(pl.program_id(1) == 0)
    def _():
        u_scr[0:CONV_HALO, :] = jnp.zeros((CONV_HALO, 2 * D_MODEL), F32)
        st_scr[...] = jnp.zeros_like(st_scr)

    u_scr[CONV_HALO:CONV_HALO + q, :] = xbc_ref[...].astype(F32)
    conv = cb_ref[...]
    for kk in range(CONV_K):
        off = CONV_HALO - (CONV_K - 1) + kk
        conv = conv + cw_ref[kk:kk + 1, :] * u_scr[off:off + q, :]
    u_scr[0:CONV_HALO, :] = u_scr[q:q + CONV_HALO, :]
    act = conv * _sigmoid(conv)
    xs = act[:, 0:D_MODEL]

    lane = lax.broadcasted_iota(jnp.int32, (1, LANES), 1)
    is_head = jnp.logical_and(lane >= N_SMALL, lane < 2 * N_SMALL)
    a_neg = jnp.where(is_head, -jnp.exp(alog_ref[...]), 0.0)
    dt = _softplus(sm_ref[...] + dtb_ref[...])
    a_cs = _exact_dot_l(_tril_bf16(q), dt * a_neg)
    ea = jnp.exp(a_cs)
    eds = jnp.exp(a_cs[q - 1:q, :] - a_cs)
    expand = e_ref[...]
    dt_x = _exact_dot_r(dt, expand)
    ea_x = _exact_dot_r(ea, expand)
    eds_x = _exact_dot_r(eds, expand)
    a_cs_t = a_cs.T

    xdt_f = xs * dt_x
    xdt = xdt_f.astype(BF16)
    xdt_s = (xdt_f * eds_x).astype(BF16)

    row = lax.broadcasted_iota(jnp.int32, (q, q), 0)
    col = lax.broadcasted_iota(jnp.int32, (q, q), 1)
    tril = col <= row
    first = lane < HEAD_DIM

    ys = []
    for g in range(SSM_GROUPS):
        lo = g * SSM_GROUP_W
        bg = act[:, D_MODEL + g * SSM_STATE:D_MODEL + (g + 1) * SSM_STATE].astype(BF16)
        cg = act[:, D_MODEL + SSM_GROUPS * SSM_STATE + g * SSM_STATE:
                 D_MODEL + SSM_GROUPS * SSM_STATE + (g + 1) * SSM_STATE].astype(BF16)
        cb = _dot_nt(cg, bg)
        state = st_scr[g]
        y_off = _dot(cg, state.astype(BF16)) * ea_x[:, lo:lo + SSM_GROUP_W]
        y_diag = []
        for pp in range(2):
            xp = xdt[:, lo + pp * LANES:lo + (pp + 1) * LANES]
            halves = []
            for hh in range(2):
                hl = N_SMALL + g * 4 + pp * 2 + hh
                seg = a_cs[:, hl:hl + 1] - a_cs_t[hl:hl + 1, :]
                decay = jnp.where(tril, jnp.exp(seg), 0.0)
                halves.append(_dot((cb * decay).astype(BF16), xp))
            y_diag.append(jnp.where(first, halves[0], halves[1]))
        new_states = lax.dot_general(bg, xdt_s[:, lo:lo + SSM_GROUP_W],
                                     (((0,), (0,)), ((), ())), preferred_element_type=F32)
        st_scr[g] = state * ea_x[q - 1:q, lo:lo + SSM_GROUP_W] + new_states
        ys.append(jnp.concatenate(y_diag, axis=1) + y_off)

    y = jnp.concatenate(ys, axis=1) + xs * dskip_ref[...]
    z = z_ref[...].astype(F32)
    yg = y * (z * _sigmoid(z))
    outs = []
    for g in range(SSM_GROUPS):
        v = yg[:, g * SSM_GROUP_W:(g + 1) * SSM_GROUP_W]
        ms = jnp.mean(v * v, axis=-1, keepdims=True)
        outs.append(v * lax.rsqrt(ms + EPS))
    o_ref[...] = (jnp.concatenate(outs, axis=1) * nw_ref[...]).astype(BF16)


def _ssd(p, small, conv_w, conv_b, dtb_pad, alog_pad, dskip_x, norm_w, expand, bsz, seq):
    q = min(SSD_CHUNK, seq)
    nc = seq // q
    const = lambda shape: pl.BlockSpec(shape, lambda b, c: (0, 0))
    return pl.pallas_call(
        functools.partial(_ssd_kernel, q=q),
        out_shape=jax.ShapeDtypeStruct((bsz * seq, D_MODEL), BF16),
        grid=(bsz, nc),
        in_specs=[
            pl.BlockSpec((q, 2 * D_MODEL), lambda b, c: (b * nc + c, C_XBC // (2 * D_MODEL))),
            pl.BlockSpec((q, D_MODEL), lambda b, c: (b * nc + c, C_SZ // D_MODEL)),
            pl.BlockSpec((q, LANES), lambda b, c: (b * nc + c, 0)),
            const((CONV_K, 2 * D_MODEL)), const((1, 2 * D_MODEL)),
            const((1, LANES)), const((1, LANES)),
            const((1, D_MODEL)), const((1, D_MODEL)),
            const((LANES, D_MODEL)),
        ],
        out_specs=pl.BlockSpec((q, D_MODEL), lambda b, c: (b * nc + c, 0)),
        scratch_shapes=[pltpu.VMEM((q + CONV_HALO, 2 * D_MODEL), F32),
                        pltpu.VMEM((SSM_GROUPS, SSM_STATE, SSM_GROUP_W), F32)],
        compiler_params=pltpu.CompilerParams(
            dimension_semantics=("arbitrary", "arbitrary"), vmem_limit_bytes=VMEM_LIMIT),
        name="ssd",
    )(p, p, small, conv_w, conv_b, dtb_pad, alog_pad, dskip_x, norm_w, expand)


def _merge_kernel(x_ref, ya_ref, yb_ref, yc_ref, g0_ref, g1_ref, g2_ref, wb_ref, wo_ref,
                  fw_ref, o_ref, *, final):
    merged = None
    for n, (y_ref, g_ref) in enumerate(((ya_ref, g0_ref), (yb_ref, g1_ref), (yc_ref, g2_ref))):
        term = _sigmoid(g_ref[...].astype(F32)) * _dot(y_ref[...], wb_ref[n])
        merged = term if merged is None else merged + term
    out = x_ref[...] + _dot(merged.astype(BF16), wo_ref[...])
    if final:
        ms = jnp.mean(out * out, axis=-1, keepdims=True)
        out = out * lax.rsqrt(ms + EPS) * fw_ref[...]
    o_ref[...] = out


def _merge(x2, y_a, y_b, y_c, p, w_branch, w_out, final_w, final):
    t = x2.shape[0]
    tm = min(MERGE_TM, t)
    rows = lambda: pl.BlockSpec((tm, D_MODEL), lambda i: (i, 0))
    gate = lambda n: pl.BlockSpec((tm, D_MODEL), lambda i, n=n: (i, C_MG // D_MODEL + n))
    return pl.pallas_call(
        functools.partial(_merge_kernel, final=final),
        out_shape=jax.ShapeDtypeStruct((t, D_MODEL), F32),
        grid=(t // tm,),
        in_specs=[rows(), rows(), rows(), rows(), gate(0), gate(1), gate(2),
                  pl.BlockSpec((3, D_MODEL, D_MODEL), lambda i: (0, 0, 0)),
                  pl.BlockSpec((D_MODEL, D_MODEL), lambda i: (0, 0)),
                  pl.BlockSpec((1, D_MODEL), lambda i: (0, 0))],
        out_specs=rows(),
        compiler_params=pltpu.CompilerParams(
            dimension_semantics=("arbitrary",), vmem_limit_bytes=VMEM_LIMIT),
        name="merge",
    )(x2, y_a, y_b, y_c, p, p, p, w_branch, w_out, final_w)


def _rope_tables(seq):
    pos = jnp.arange(seq, dtype=F32)
    inv_freq = ROPE_THETA ** (-jnp.arange(0, 2 * ROT_HALF, 2, dtype=F32) / (2 * ROT_HALF))
    ang = pos[:, None] * inv_freq[None, :]
    cos, sin = jnp.cos(ang), jnp.sin(ang)
    ones = jnp.ones((seq, HEAD_DIM - 2 * ROT_HALF), F32)
    zeros = jnp.zeros((seq, ROT_HALF), F32)
    rest = jnp.zeros((seq, HEAD_DIM - 2 * ROT_HALF), F32)
    cos_c = jnp.concatenate([cos, cos, ones], axis=1)
    s1_c = jnp.concatenate([-sin, zeros, rest], axis=1)
    s2_c = jnp.concatenate([zeros, sin, rest], axis=1)
    two = lambda t: jnp.concatenate([t, t], axis=1)
    return two(cos_c), two(s1_c), two(s2_c)


def _pad_lanes(v, start):
    return jnp.zeros((1, LANES), F32).at[0, start:start + v.shape[0]].set(v.astype(F32))


def _layer_weights(w_in):
    o = np.cumsum([0, 1024, 1024, 1024, 16, 1024, 1024, 2048, 16, 1024, 1024, 1024, 1024, 3072])
    seg = lambda i: w_in[:, o[i]:o[i + 1]]
    scale = HEAD_DIM ** -0.5
    w_main = jnp.concatenate(
        [seg(0) * scale, seg(1), seg(2), seg(4), seg(6), seg(5),
         seg(8) * scale, seg(9), seg(10), seg(11), seg(12)], axis=1).astype(BF16)
    w_small = jnp.concatenate(
        [seg(3), seg(7), jnp.zeros((D_MODEL, LANES - 2 * N_SMALL), F32)], axis=1).astype(BF16)
    return w_main, w_small


def kernel(x, norm_w, w_in, b_forget, conv_w, conv_b, dt_bias, a_log, d_skip, ssm_norm_w,
           diff_lambda, subln_w, w_branch, w_out, final_norm_w):
    bsz, seq, _ = x.shape
    depth = norm_w.shape[0]
    cos_t, s1_t, s2_t = _rope_tables(seq)
    heads = jnp.arange(D_MODEL) // HEAD_DIM
    expand = (jnp.arange(LANES)[:, None] == (heads[None, :] + N_SMALL)).astype(BF16)

    x2 = x.reshape(bsz * seq, D_MODEL)
    for layer in range(depth):
        w_main, w_small = _layer_weights(w_in[layer])
        p, small = _inproj(x2, norm_w[layer][None, :], w_main, w_small)

        ft = _fprep(small, _pad_lanes(b_forget[layer], 0), bsz, seq)
        y_a = _fox(p, ft.reshape(bsz, N_SMALL // 2, 2, seq), bsz, seq)

        y_b = _ssd(p, small, conv_w[layer], conv_b[layer][None, :],
                   _pad_lanes(dt_bias[layer], N_SMALL),
                   _pad_lanes(a_log[layer], N_SMALL),
                   jnp.repeat(d_skip[layer], HEAD_DIM)[None, :],
                   ssm_norm_w[layer][None, :], expand, bsz, seq)

        lam_init = 0.8 - 0.6 * math.exp(-0.3 * layer)
        y_c = _diff(p, cos_t, s1_t, s2_t, diff_lambda[layer], subln_w[layer][None, :],
                    bsz, seq, lam_init)

        x2 = _merge(x2, y_a, y_b, y_c, p, w_branch[layer].astype(BF16),
                    w_out[layer].astype(BF16), final_norm_w[None, :], layer == depth - 1)
    return x2.reshape(bsz, seq, D_MODEL)
```

```python
import functools
import math

import numpy as np
import jax
import jax.numpy as jnp
from jax import lax
from jax.experimental import pallas as pl
from jax.experimental.pallas import tpu as pltpu

F32 = jnp.float32
BF16 = jnp.bfloat16

D_MODEL = 1024
EPS = 1e-6
HEAD_DIM = 64
LANES = 128
N_SMALL = 16
SSM_GROUPS = 4
SSM_STATE = 128
SSM_GROUP_W = 256
CONV_K = 4
CONV_HALO = 8
DIFF_HEADS = 8
ROT_HALF = 8
ROPE_THETA = 500000.0
MASK_CHUNK = 64

C_FQ, C_FK, C_FV, C_FG = 0, 1024, 2048, 3072
C_XBC, C_SZ = 4096, 6144
C_DQ, C_DK, C_DV, C_DG = 7168, 8192, 9216, 10240
C_MG = 11264
N_MAIN = 14336

ATT_TILE = 256
SSD_CHUNK = 128
PROJ_TM, PROJ_TN = 1024, 1024
MERGE_TM = 512
VMEM_LIMIT = 56 * 1024 * 1024


def _sigmoid(x):
    return 1.0 / (1.0 + jnp.exp(-x))


def _softplus(x):
    return jnp.maximum(x, 0.0) + jnp.log1p(jnp.exp(-jnp.abs(x)))


def _split3(x):
    hi = x.astype(BF16)
    r1 = x - hi.astype(F32)
    mid = r1.astype(BF16)
    lo = (r1 - mid.astype(F32)).astype(BF16)
    return hi, mid, lo


def _dot(a, b):
    return jnp.dot(a, b, preferred_element_type=F32)


def _dot_nt(a, b):
    return lax.dot_general(a, b, (((1,), (1,)), ((), ())), preferred_element_type=F32)


def _exact_dot_r(x, m01):
    hi, mid, lo = _split3(x)
    return _dot(hi, m01) + _dot(mid, m01) + _dot(lo, m01)


def _exact_dot_l(m01, x):
    hi, mid, lo = _split3(x)
    return _dot(m01, hi) + _dot(m01, mid) + _dot(m01, lo)


def _tril_bf16(n):
    r = lax.broadcasted_iota(jnp.int32, (n, n), 0)
    c = lax.broadcasted_iota(jnp.int32, (n, n), 1)
    return jnp.where(r >= c, 1.0, 0.0).astype(BF16)


def _inproj_kernel(x_ref, nw_ref, w_ref, ws_ref, p_ref, s_ref, h_ref):
    @pl.when(pl.program_id(1) == 0)
    def _():
        x = x_ref[...]
        ms = jnp.mean(x * x, axis=-1, keepdims=True)
        h = (x * lax.rsqrt(ms + EPS) * nw_ref[...]).astype(BF16)
        h_ref[...] = h
        s_ref[...] = _dot(h, ws_ref[...])

    p_ref[...] = _dot(h_ref[...], w_ref[...]).astype(BF16)


def _inproj(x2, norm_w, w_main, w_small):
    t = x2.shape[0]
    tm, tn = min(PROJ_TM, t), PROJ_TN
    return pl.pallas_call(
        _inproj_kernel,
        out_shape=(jax.ShapeDtypeStruct((t, N_MAIN), BF16),
                   jax.ShapeDtypeStruct((t, LANES), F32)),
        grid=(t // tm, N_MAIN // tn),
        in_specs=[
            pl.BlockSpec((tm, D_MODEL), lambda i, j: (i, 0)),
            pl.BlockSpec((1, D_MODEL), lambda i, j: (0, 0)),
            pl.BlockSpec((D_MODEL, tn), lambda i, j: (0, j)),
            pl.BlockSpec((D_MODEL, LANES), lambda i, j: (0, 0)),
        ],
        out_specs=(pl.BlockSpec((tm, tn), lambda i, j: (i, j)),
                   pl.BlockSpec((tm, LANES), lambda i, j: (i, 0))),
        scratch_shapes=[pltpu.VMEM((tm, D_MODEL), BF16)],
        compiler_params=pltpu.CompilerParams(
            dimension_semantics=("arbitrary", "arbitrary"), vmem_limit_bytes=VMEM_LIMIT),
        name="inproj",
    )(x2, norm_w, w_main, w_small)


def _fprep_kernel(s_ref, bias_ref, ft_ref, f_scr, *, seq):
    tril = _tril_bf16(LANES)
    carry = jnp.zeros((1, LANES), F32)
    for blk in range(seq // LANES):
        x = s_ref[blk * LANES:(blk + 1) * LANES, :] + bias_ref[...]
        log_f = jnp.minimum(x, 0.0) - jnp.log1p(jnp.exp(-jnp.abs(x)))
        c = _exact_dot_l(tril, log_f) + carry
        f_scr[blk * LANES:(blk + 1) * LANES, :] = c
        carry = c[LANES - 1:LANES, :]
    ft_ref[...] = f_scr[...].T[0:N_SMALL, :]


def _fprep(small, bias_pad, bsz, seq):
    return pl.pallas_call(
        functools.partial(_fprep_kernel, seq=seq),
        out_shape=jax.ShapeDtypeStruct((bsz, N_SMALL, seq), F32),
        grid=(bsz,),
        in_specs=[pl.BlockSpec((seq, LANES), lambda b: (b, 0)),
                  pl.BlockSpec((1, LANES), lambda b: (0, 0))],
        out_specs=pl.BlockSpec((None, N_SMALL, seq), lambda b: (b, 0, 0)),
        scratch_shapes=[pltpu.VMEM((seq, LANES), F32)],
        compiler_params=pltpu.CompilerParams(
            dimension_semantics=("arbitrary",), vmem_limit_bytes=VMEM_LIMIT),
        name="fprep",
    )(small, bias_pad)


def _lane_tiles(x, op):
    t = x[:, 0:LANES]
    for w in range(1, x.shape[1] // LANES):
        t = op(t, x[:, w * LANES:(w + 1) * LANES])
    return t


def _attend(qm, kt_ref, v_ref, s_scr, i, tile, diag_mask, bias_row):
    mrun = None
    for j in range(i + 1):
        c0 = j * tile
        s = _dot(qm, kt_ref[:, c0:c0 + tile])
        if bias_row is not None:
            s = s - bias_row[:, c0:c0 + tile]
        if j == i:
            s = jnp.where(diag_mask, s, -jnp.inf)
        s_scr[:, c0:c0 + tile] = s
        t = _lane_tiles(s, jnp.maximum)
        mrun = t if mrun is None else jnp.maximum(mrun, t)
    m = jnp.max(mrun, axis=1, keepdims=True)
    lsum = acc = None
    for j in range(i + 1):
        c0 = j * tile
        p = jnp.exp(s_scr[:, c0:c0 + tile] - m)
        t = _lane_tiles(p, jnp.add)
        pv = _dot(p.astype(BF16), v_ref[c0:c0 + tile, :])
        lsum = t if lsum is None else lsum + t
        acc = pv if acc is None else acc + pv
    return acc, jnp.sum(lsum, axis=1, keepdims=True)


def _transpose_keys(k_ref, kt_scr, seq, tile, fn=None):
    for r0 in range(0, seq, tile):
        x = k_ref[r0:r0 + tile, :].astype(F32)
        if fn is not None:
            x = fn(x, r0)
        kt_scr[:, r0:r0 + tile] = x.T.astype(BF16)


def _fox_kernel(q_ref, k_ref, v_ref, g_ref, ft_ref, o_ref, kt_scr, s_scr, *, seq, tile):
    lane = lax.broadcasted_iota(jnp.int32, (1, LANES), 1)
    row = lax.broadcasted_iota(jnp.int32, (tile, tile), 0)
    col = lax.broadcasted_iota(jnp.int32, (tile, tile), 1)
    causal = col <= row
    first = lane < HEAD_DIM
    _transpose_keys(k_ref, kt_scr, seq, tile)

    for i in range(seq // tile):
        r0 = i * tile
        q = q_ref[r0:r0 + tile, :]
        outs = []
        for hh in range(2):
            qm = jnp.where(first if hh == 0 else jnp.logical_not(first), q, jnp.zeros_like(q))
            acc, l = _attend(qm, kt_scr, v_ref, s_scr.at[hh], i, tile, causal,
                             ft_ref.at[hh:hh + 1])
            outs.append(acc / l)
        o = jnp.where(first, outs[0], outs[1])
        g = g_ref[r0:r0 + tile, :].astype(F32)
        o_ref[r0:r0 + tile, :] = (o * (g * _sigmoid(g))).astype(BF16)


def _fox(p, ft4, bsz, seq):
    tile = min(ATT_TILE, seq)
    blk = lambda off: pl.BlockSpec((seq, LANES), lambda b, h, off=off: (b, off // LANES + h))
    return pl.pallas_call(
        functools.partial(_fox_kernel, seq=seq, tile=tile),
        out_shape=jax.ShapeDtypeStruct((bsz * seq, D_MODEL), BF16),
        grid=(bsz, D_MODEL // LANES),
        in_specs=[blk(C_FQ), blk(C_FK), blk(C_FV), blk(C_FG),
                  pl.BlockSpec((None, None, 2, seq), lambda b, h: (b, h, 0, 0))],
        out_specs=pl.BlockSpec((seq, LANES), lambda b, h: (b, h)),
        scratch_shapes=[pltpu.VMEM((LANES, seq), BF16), pltpu.VMEM((2, tile, seq), F32)],
        compiler_params=pltpu.CompilerParams(
            dimension_semantics=("arbitrary", "arbitrary"), vmem_limit_bytes=VMEM_LIMIT),
        name="fox",
    )(p, p, p, p, ft4)


def _diff_kernel(q_ref, k_ref, v_ref, g_ref, cos_ref, s1_ref, s2_ref, lam_ref, sub_ref,
                 o_ref, kt_scr, s_scr, *, seq, tile, lam_init):
    def rope(x, r0):
        return (x * cos_ref[r0:r0 + tile, :]
                + pltpu.roll(x, LANES - ROT_HALF, 1) * s1_ref[r0:r0 + tile, :]
                + pltpu.roll(x, ROT_HALF, 1) * s2_ref[r0:r0 + tile, :])

    _transpose_keys(k_ref, kt_scr, seq, tile, rope)

    lp = lam_ref[...]
    lam = (jnp.exp(jnp.sum(lp[0:1] * lp[1:2], axis=1, keepdims=True))
           - jnp.exp(jnp.sum(lp[2:3] * lp[3:4], axis=1, keepdims=True)) + lam_init)

    lane = lax.broadcasted_iota(jnp.int32, (1, LANES), 1)
    first = lane < HEAD_DIM
    row = lax.broadcasted_iota(jnp.int32, (tile, tile), 0)
    col = lax.broadcasted_iota(jnp.int32, (tile, tile), 1)
    chunk_causal = (col // MASK_CHUNK) <= (row // MASK_CHUNK)

    for i in range(seq // tile):
        r0 = i * tile
        q = rope(q_ref[r0:r0 + tile, :].astype(F32), r0).astype(BF16)
        outs = []
        for cc in range(2):
            qm = jnp.where(first if cc == 0 else jnp.logical_not(first), q, jnp.zeros_like(q))
            acc, l = _attend(qm, kt_scr, v_ref, s_scr.at[cc], i, tile, chunk_causal, None)
            outs.append(acc / l)
        o = outs[0] - lam * outs[1]
        ms = jnp.mean(o * o, axis=-1, keepdims=True)
        o = o * lax.rsqrt(ms + EPS) * sub_ref[...] * (1.0 - lam_init)
        g = g_ref[r0:r0 + tile, :].astype(F32)
        o_ref[r0:r0 + tile, :] = (o * (g * _sigmoid(g))).astype(BF16)


def _diff(p, cos_t, s1_t, s2_t, lam_p, sub_w, bsz, seq, lam_init):
    tile = min(ATT_TILE, seq)
    blk = lambda off: pl.BlockSpec((seq, LANES), lambda b, h, off=off: (b, off // LANES + h))
    const = lambda shape: pl.BlockSpec(shape, lambda b, h: (0, 0))
    return pl.pallas_call(
        functools.partial(_diff_kernel, seq=seq, tile=tile, lam_init=lam_init),
        out_shape=jax.ShapeDtypeStruct((bsz * seq, D_MODEL), BF16),
        grid=(bsz, DIFF_HEADS),
        in_specs=[blk(C_DQ), blk(C_DK), blk(C_DV), blk(C_DG),
                  const((seq, LANES)), const((seq, LANES)), const((seq, LANES)),
                  const((4, HEAD_DIM)), const((1, LANES))],
        out_specs=pl.BlockSpec((seq, LANES), lambda b, h: (b, h)),
        scratch_shapes=[pltpu.VMEM((LANES, seq), BF16), pltpu.VMEM((2, tile, seq), F32)],
        compiler_params=pltpu.CompilerParams(
            dimension_semantics=("arbitrary", "arbitrary"), vmem_limit_bytes=VMEM_LIMIT),
        name="diffattn",
    )(p, p, p, p, cos_t, s1_t, s2_t, lam_p, sub_w)


def _ssd_kernel(xbc_ref, z_ref, sm_ref, cw_ref, cb_ref, dtb_ref, alog_ref, dskip_ref, nw_ref,
                e_ref, o_ref, u_scr, st_scr, *, q):
    @pl.when(pl.program_id(1) == 0)
    def _():
        u_scr[0:CONV_HALO, :] = jnp.zeros((CONV_HALO, 2 * D_MODEL), F32)
        st_scr[...] = jnp.zeros_like(st_scr)

    u_scr[CONV_HALO:CONV_HALO + q, :] = xbc_ref[...].astype(F32)
    conv = cb_ref[...]
    for kk in range(CONV_K):
        off = CONV_HALO - (CONV_K - 1) + kk
        conv = conv + cw_ref[kk:kk + 1, :] * u_scr[off:off + q, :]
    u_scr[0:CONV_HALO, :] = u_scr[q:q + CONV_HALO, :]
    act = conv * _sigmoid(conv)
    xs = act[:, 0:D_MODEL]

    lane = lax.broadcasted_iota(jnp.int32, (1, LANES), 1)
    is_head = jnp.logical_and(lane >= N_SMALL, lane < 2 * N_SMALL)
    a_neg = jnp.where(is_head, -jnp.exp(alog_ref[...]), 0.0)
    dt = _softplus(sm_ref[...] + dtb_ref[...])
    a_cs = _exact_dot_l(_tril_bf16(q), dt * a_neg)
    ea = jnp.exp(a_cs)
    eds = jnp.exp(a_cs[q - 1:q, :] - a_cs)
    expand = e_ref[...]
    dt_x = _exact_dot_r(dt, expand)
    ea_x = _exact_dot_r(ea, expand)
    eds_x = _exact_dot_r(eds, expand)
    a_cs_t = a_cs.T

    xdt_f = xs * dt_x
    xdt = xdt_f.astype(BF16)
    xdt_s = (xdt_f * eds_x).astype(BF16)

    row = lax.broadcasted_iota(jnp.int32, (q, q), 0)
    col = lax.broadcasted_iota(jnp.int32, (q, q), 1)
    tril = col <= row
    first = lane < HEAD_DIM

    ys = []
    for g in range(SSM_GROUPS):
        lo = g * SSM_GROUP_W
        bg = act[:, D_MODEL + g * SSM_STATE:D_MODEL + (g + 1) * SSM_STATE].astype(BF16)
        cg = act[:, D_MODEL + SSM_GROUPS * SSM_STATE + g * SSM_STATE:
                 D_MODEL + SSM_GROUPS * SSM_STATE + (g + 1) * SSM_STATE].astype(BF16)
        cb = _dot_nt(cg, bg)
        state = st_scr[g]
        y_off = _dot(cg, state.astype(BF16)) * ea_x[:, lo:lo + SSM_GROUP_W]
        y_diag = []
        for pp in range(2):
            xp = xdt[:, lo + pp * LANES:lo + (pp + 1) * LANES]
            halves = []
            for hh in range(2):
                hl = N_SMALL + g * 4 + pp * 2 + hh
                seg = a_cs[:, hl:hl + 1] - a_cs_t[hl:hl + 1, :]
                decay = jnp.where(tril, jnp.exp(seg), 0.0)
                halves.append(_dot((cb * decay).astype(BF16), xp))
            y_diag.append(jnp.where(first, halves[0], halves[1]))
        new_states = lax.dot_general(bg, xdt_s[:, lo:lo + SSM_GROUP_W],
                                     (((0,), (0,)), ((), ())), preferred_element_type=F32)
        st_scr[g] = state * ea_x[q - 1:q, lo:lo + SSM_GROUP_W] + new_states
        ys.append(jnp.concatenate(y_diag, axis=1) + y_off)

    y = jnp.concatenate(ys, axis=1) + xs * dskip_ref[...]
    z = z_ref[...].astype(F32)
    yg = y * (z * _sigmoid(z))
    outs = []
    for g in range(SSM_GROUPS):
        v = yg[:, g * SSM_GROUP_W:(g + 1) * SSM_GROUP_W]
        ms = jnp.mean(v * v, axis=-1, keepdims=True)
        outs.append(v * lax.rsqrt(ms + EPS))
    o_ref[...] = (jnp.concatenate(outs, axis=1) * nw_ref[...]).astype(BF16)


def _ssd(p, small, conv_w, conv_b, dtb_pad, alog_pad, dskip_x, norm_w, expand, bsz, seq):
    q = min(SSD_CHUNK, seq)
    nc = seq // q
    const = lambda shape: pl.BlockSpec(shape, lambda b, c: (0, 0))
    return pl.pallas_call(
        functools.partial(_ssd_kernel, q=q),
        out_shape=jax.ShapeDtypeStruct((bsz * seq, D_MODEL), BF16),
        grid=(bsz, nc),
        in_specs=[
            pl.BlockSpec((q, 2 * D_MODEL), lambda b, c: (b * nc + c, C_XBC // (2 * D_MODEL))),
            pl.BlockSpec((q, D_MODEL), lambda b, c: (b * nc + c, C_SZ // D_MODEL)),
            pl.BlockSpec((q, LANES), lambda b, c: (b * nc + c, 0)),
            const((CONV_K, 2 * D_MODEL)), const((1, 2 * D_MODEL)),
            const((1, LANES)), const((1, LANES)),
            const((1, D_MODEL)), const((1, D_MODEL)),
            const((LANES, D_MODEL)),
        ],
        out_specs=pl.BlockSpec((q, D_MODEL), lambda b, c: (b * nc + c, 0)),
        scratch_shapes=[pltpu.VMEM((q + CONV_HALO, 2 * D_MODEL), F32),
                        pltpu.VMEM((SSM_GROUPS, SSM_STATE, SSM_GROUP_W), F32)],
        compiler_params=pltpu.CompilerParams(
            dimension_semantics=("arbitrary", "arbitrary"), vmem_limit_bytes=VMEM_LIMIT),
        name="ssd",
    )(p, p, small, conv_w, conv_b, dtb_pad, alog_pad, dskip_x, norm_w, expand)


def _merge_kernel(x_ref, ya_ref, yb_ref, yc_ref, g0_ref, g1_ref, g2_ref, wb_ref, wo_ref,
                  fw_ref, o_ref, *, final):
    merged = None
    for n, (y_ref, g_ref) in enumerate(((ya_ref, g0_ref), (yb_ref, g1_ref), (yc_ref, g2_ref))):
        term = _sigmoid(g_ref[...].astype(F32)) * _dot(y_ref[...], wb_ref[n])
        merged = term if merged is None else merged + term
    out = x_ref[...] + _dot(merged.astype(BF16), wo_ref[...])
    if final:
        ms = jnp.mean(out * out, axis=-1, keepdims=True)
        out = out * lax.rsqrt(ms + EPS) * fw_ref[...]
    o_ref[...] = out


def _merge(x2, y_a, y_b, y_c, p, w_branch, w_out, final_w, final):
    t = x2.shape[0]
    tm = min(MERGE_TM, t)
    rows = lambda: pl.BlockSpec((tm, D_MODEL), lambda i: (i, 0))
    gate = lambda n: pl.BlockSpec((tm, D_MODEL), lambda i, n=n: (i, C_MG // D_MODEL + n))
    return pl.pallas_call(
        functools.partial(_merge_kernel, final=final),
        out_shape=jax.ShapeDtypeStruct((t, D_MODEL), F32),
        grid=(t // tm,),
        in_specs=[rows(), rows(), rows(), rows(), gate(0), gate(1), gate(2),
                  pl.BlockSpec((3, D_MODEL, D_MODEL), lambda i: (0, 0, 0)),
                  pl.BlockSpec((D_MODEL, D_MODEL), lambda i: (0, 0)),
                  pl.BlockSpec((1, D_MODEL), lambda i: (0, 0))],
        out_specs=rows(),
        compiler_params=pltpu.CompilerParams(
            dimension_semantics=("arbitrary",), vmem_limit_bytes=VMEM_LIMIT),
        name="merge",
    )(x2, y_a, y_b, y_c, p, p, p, w_branch, w_out, final_w)


def _rope_tables(seq):
    pos = jnp.arange(seq, dtype=F32)
    inv_freq = ROPE_THETA ** (-jnp.arange(0, 2 * ROT_HALF, 2, dtype=F32) / (2 * ROT_HALF))
    ang = pos[:, None] * inv_freq[None, :]
    cos, sin = jnp.cos(ang), jnp.sin(ang)
    ones = jnp.ones((seq, HEAD_DIM - 2 * ROT_HALF), F32)
    zeros = jnp.zeros((seq, ROT_HALF), F32)
    rest = jnp.zeros((seq, HEAD_DIM - 2 * ROT_HALF), F32)
    cos_c = jnp.concatenate([cos, cos, ones], axis=1)
    s1_c = jnp.concatenate([-sin, zeros, rest], axis=1)
    s2_c = jnp.concatenate([zeros, sin, rest], axis=1)
    two = lambda t: jnp.concatenate([t, t], axis=1)
    return two(cos_c), two(s1_c), two(s2_c)


def _pad_lanes(v, start):
    return jnp.zeros((1, LANES), F32).at[0, start:start + v.shape[0]].set(v.astype(F32))


def _layer_weights(w_in):
    o = np.cumsum([0, 1024, 1024, 1024, 16, 1024, 1024, 2048, 16, 1024, 1024, 1024, 1024, 3072])
    seg = lambda i: w_in[:, o[i]:o[i + 1]]
    scale = HEAD_DIM ** -0.5
    w_main = jnp.concatenate(
        [seg(0) * scale, seg(1), seg(2), seg(4), seg(6), seg(5),
         seg(8) * scale, seg(9), seg(10), seg(11), seg(12)], axis=1).astype(BF16)
    w_small = jnp.concatenate(
        [seg(3), seg(7), jnp.zeros((D_MODEL, LANES - 2 * N_SMALL), F32)], axis=1).astype(BF16)
    return w_main, w_small


def kernel(x, norm_w, w_in, b_forget, conv_w, conv_b, dt_bias, a_log, d_skip, ssm_norm_w,
           diff_lambda, subln_w, w_branch, w_out, final_norm_w):
    bsz, seq, _ = x.shape
    depth = norm_w.shape[0]
    cos_t, s1_t, s2_t = _rope_tables(seq)
    heads = jnp.arange(D_MODEL) // HEAD_DIM
    expand = (jnp.arange(LANES)[:, None] == (heads[None, :] + N_SMALL)).astype(BF16)

    x2 = x.reshape(bsz * seq, D_MODEL)
    for layer in range(depth):
        w_main, w_small = _layer_weights(w_in[layer])
        p, small = _inproj(x2, norm_w[layer][None, :], w_main, w_small)

        ft = _fprep(small, _pad_lanes(b_forget[layer], 0), bsz, seq)
        y_a = _fox(p, ft.reshape(bsz, N_SMALL // 2, 2, seq), bsz, seq)

        y_b = _ssd(p, small, conv_w[layer], conv_b[layer][None, :],
                   _pad_lanes(dt_bias[layer], N_SMALL),
                   _pad_lanes(a_log[layer], N_SMALL),
                   jnp.repeat(d_skip[layer], HEAD_DIM)[None, :],
                   ssm_norm_w[layer][None, :], expand, bsz, seq)

        lam_init = 0.8 - 0.6 * math.exp(-0.3 * layer)
        y_c = _diff(p, cos_t, s1_t, s2_t, diff_lambda[layer], subln_w[layer][None, :],
                    bsz, seq, lam_init)

        x2 = _merge(x2, y_a, y_b, y_c, p, w_branch[layer].astype(BF16),
                    w_out[layer].astype(BF16), final_norm_w[None, :], layer == depth - 1)
    return x2.reshape(bsz, seq, D_MODEL)
```

```python
import functools
import math

import numpy as np
import jax
import jax.numpy as jnp
from jax import lax
from jax.experimental import pallas as pl
from jax.experimental.pallas import tpu as pltpu

F32 = jnp.float32
BF16 = jnp.bfloat16

D_MODEL = 1024
EPS = 1e-6
LOG2E = 1.4426950408889634
HEAD_DIM = 64
LANES = 128
N_SMALL = 16
SSM_GROUPS = 4
SSM_STATE = 128
SSM_GROUP_W = 256
CONV_K = 4
CONV_HALO = 16
CONV_PAD = 128
DIFF_HEADS = 8
ROT_HALF = 8
ROPE_THETA = 500000.0
MASK_CHUNK = 64

C_FQ, C_FK, C_FV, C_FG = 0, 1024, 2048, 3072
C_XBC, C_SZ = 4096, 6144
C_DQ, C_DK, C_DV, C_DG = 7168, 8192, 9216, 10240
C_MG = 11264
N_MAIN = 14336

ATT_TILE = 256
SSD_CHUNK = 128
SSD_SUBCHUNKS = 2
PROJ_TM, PROJ_TN = 2048, 1024
MERGE_TM = 512
VMEM_LIMIT = 56 * 1024 * 1024


def _sigmoid(x):
    return 0.5 * jnp.tanh(0.5 * x) + 0.5


def _softplus(x):
    return jnp.maximum(x, 0.0) + jnp.log1p(jnp.exp(-jnp.abs(x)))


def _split3(x):
    hi = x.astype(BF16)
    r1 = x - hi.astype(F32)
    mid = r1.astype(BF16)
    lo = (r1 - mid.astype(F32)).astype(BF16)
    return hi, mid, lo


def _dot(a, b):
    return jnp.dot(a, b, preferred_element_type=F32)


def _dot_nt(a, b):
    return lax.dot_general(a, b, (((1,), (1,)), ((), ())), preferred_element_type=F32)


def _exact_dot_l(m01, x):
    hi, mid, lo = _split3(x)
    return _dot(m01, hi) + _dot(m01, mid) + _dot(m01, lo)


def _tril_bf16(n):
    r = lax.broadcasted_iota(jnp.int32, (n, n), 0)
    c = lax.broadcasted_iota(jnp.int32, (n, n), 1)
    return jnp.where(r >= c, 1.0, 0.0).astype(BF16)


def _inproj_kernel(x_ref, nw_ref, w_ref, ws_ref, p_ref, s_ref, h_ref):
    @pl.when(pl.program_id(1) == 0)
    def _():
        x = x_ref[...]
        ms = jnp.mean(x * x, axis=-1, keepdims=True)
        h = (x * lax.rsqrt(ms + EPS) * nw_ref[...]).astype(BF16)
        h_ref[...] = h
        s_ref[...] = _dot(h, ws_ref[...])

    p_ref[...] = _dot(h_ref[...], w_ref[...]).astype(BF16)


def _inproj(x2, norm_w, w_main, w_small):
    t = x2.shape[0]
    tm, tn = min(PROJ_TM, t), PROJ_TN
    return pl.pallas_call(
        _inproj_kernel,
        out_shape=(jax.ShapeDtypeStruct((t, N_MAIN), BF16),
                   jax.ShapeDtypeStruct((t, LANES), F32)),
        grid=(t // tm, N_MAIN // tn),
        in_specs=[
            pl.BlockSpec((tm, D_MODEL), lambda i, j: (i, 0)),
            pl.BlockSpec((1, D_MODEL), lambda i, j: (0, 0)),
            pl.BlockSpec((D_MODEL, tn), lambda i, j: (0, j)),
            pl.BlockSpec((D_MODEL, LANES), lambda i, j: (0, 0)),
        ],
        out_specs=(pl.BlockSpec((tm, tn), lambda i, j: (i, j)),
                   pl.BlockSpec((tm, LANES), lambda i, j: (i, 0))),
        scratch_shapes=[pltpu.VMEM((tm, D_MODEL), BF16)],
        compiler_params=pltpu.CompilerParams(
            dimension_semantics=("arbitrary", "arbitrary"), vmem_limit_bytes=VMEM_LIMIT),
        name="inproj",
    )(x2, norm_w, w_main, w_small)


def _fprep_kernel(s_ref, bias_ref, ft_ref, f_scr, *, seq):
    tril = _tril_bf16(LANES)
    carry = jnp.zeros((1, LANES), F32)
    for blk in range(seq // LANES):
        x = s_ref[blk * LANES:(blk + 1) * LANES, :] + bias_ref[...]
        log_f = jnp.minimum(x, 0.0) - jnp.log1p(jnp.exp(-jnp.abs(x)))
        c = _exact_dot_l(tril, log_f) + carry
        f_scr[blk * LANES:(blk + 1) * LANES, :] = c
        carry = c[LANES - 1:LANES, :]
    ft_ref[...] = f_scr[...].T[0:N_SMALL, :]


def _fprep(small, bias_pad, bsz, seq):
    return pl.pallas_call(
        functools.partial(_fprep_kernel, seq=seq),
        out_shape=jax.ShapeDtypeStruct((bsz, N_SMALL, seq), F32),
        grid=(bsz,),
        in_specs=[pl.BlockSpec((seq, LANES), lambda b: (b, 0)),
                  pl.BlockSpec((1, LANES), lambda b: (0, 0))],
        out_specs=pl.BlockSpec((None, N_SMALL, seq), lambda b: (b, 0, 0)),
        scratch_shapes=[pltpu.VMEM((seq, LANES), F32)],
        compiler_params=pltpu.CompilerParams(
            dimension_semantics=("arbitrary",), vmem_limit_bytes=VMEM_LIMIT),
        name="fprep",
    )(small, bias_pad)


def _lane_tiles(x, op):
    t = x[:, 0:LANES]
    for w in range(1, x.shape[1] // LANES):
        t = op(t, x[:, w * LANES:(w + 1) * LANES])
    return t


def _score_pass(qms, kt_refs, s_refs, i, tile, diag_mask):
    mruns = [None] * len(qms)
    for j in range(i + 1):
        c0 = j * tile
        for h, qm in enumerate(qms):
            s = _dot(qm, kt_refs[h][:, c0:c0 + tile])
            if j == i:
                s = jnp.where(diag_mask, s, -jnp.inf)
            s_refs[h][:, c0:c0 + tile] = s
            t = _lane_tiles(s, jnp.maximum)
            mruns[h] = t if mruns[h] is None else jnp.maximum(mruns[h], t)
    return [jnp.max(t, axis=1, keepdims=True) for t in mruns]


def _weight_pass(ms, s_refs, v_refs, i, tile, row_sums):
    n = len(ms)
    accs, lsums = [None] * n, [None] * n
    for j in range(i + 1):
        c0 = j * tile
        for h in range(n):
            p = jnp.exp2(s_refs[h][:, c0:c0 + tile] - ms[h])
            pv = _dot(p.astype(BF16), v_refs[h][c0:c0 + tile, :])
            accs[h] = pv if accs[h] is None else accs[h] + pv
            if row_sums:
                t = _lane_tiles(p, jnp.add)
                lsums[h] = t if lsums[h] is None else lsums[h] + t
    sums = [jnp.sum(t, axis=1, keepdims=True) for t in lsums] if row_sums else None
    return accs, sums


def _two_pass_sweep(n_tiles, make_queries, score_args, weight_args, finish):
    kt_refs, s_scr, tile, mask = score_args
    v_refs, row_sums = weight_args
    bufs = lambda i: [s_scr.at[i % 2, h] for h in range(2)]
    ms = _score_pass(make_queries(0), kt_refs, bufs(0), 0, tile, mask)
    for i in range(n_tiles):
        ms_next = None
        if i + 1 < n_tiles:
            ms_next = _score_pass(make_queries(i + 1), kt_refs, bufs(i + 1), i + 1, tile, mask)
        accs, sums = _weight_pass(ms, bufs(i), v_refs, i, tile, row_sums)
        finish(i, accs, sums)
        ms = ms_next


def _fox_kernel(q_ref, k_ref, v_ref, g_ref, ft_ref, o_ref, kt_scr, v_scr, s_scr, *, seq, tile):
    lane = lax.broadcasted_iota(jnp.int32, (1, LANES), 1)
    row = lax.broadcasted_iota(jnp.int32, (tile, tile), 0)
    col = lax.broadcasted_iota(jnp.int32, (tile, tile), 1)
    causal = col <= row
    first = lane < HEAD_DIM
    own = (first, jnp.logical_not(first))
    bias_lo = (HEAD_DIM, 0)
    ones_q = tuple(jnp.logical_and(lane >= b, lane < b + 3).astype(F32) for b in bias_lo)

    for r0 in range(0, seq, tile):
        kt = k_ref[r0:r0 + tile, :].astype(F32).T
        v = v_ref[r0:r0 + tile, :]
        for hh in range(2):
            hi, mid, lo = _split3(ft_ref[hh:hh + 1, r0:r0 + tile] * (-LOG2E))
            rows = jnp.concatenate([hi.astype(F32), mid.astype(F32), lo.astype(F32),
                                    jnp.zeros((5, tile), F32)], axis=0)
            b = bias_lo[hh]
            kta = jnp.concatenate(([kt[0:b]] if b else []) + [rows, kt[b + 8:LANES]], axis=0)
            kt_scr[hh, :, r0:r0 + tile] = kta.astype(BF16)
            v_scr[hh, r0:r0 + tile, :] = jnp.where(own[hh], v, jnp.ones_like(v))

    def queries(i):
        q = q_ref[i * tile:(i + 1) * tile, :].astype(F32)
        return [jnp.where(own[hh], q, ones_q[hh]).astype(BF16) for hh in range(2)]

    def finish(i, accs, _):
        r0 = i * tile
        num = jnp.where(first, accs[0], accs[1])
        den = pltpu.roll(jnp.where(first, accs[1], accs[0]), HEAD_DIM, 1)
        g = g_ref[r0:r0 + tile, :].astype(F32)
        o_ref[r0:r0 + tile, :] = (num / den * (g * _sigmoid(g))).astype(BF16)

    _two_pass_sweep(seq // tile, queries,
                    ([kt_scr.at[0], kt_scr.at[1]], s_scr, tile, causal),
                    ([v_scr.at[0], v_scr.at[1]], False), finish)


def _fox(p, ft4, bsz, seq):
    tile = min(ATT_TILE, seq)
    blk = lambda off: pl.BlockSpec((seq, LANES), lambda b, h, off=off: (b, off // LANES + h))
    return pl.pallas_call(
        functools.partial(_fox_kernel, seq=seq, tile=tile),
        out_shape=jax.ShapeDtypeStruct((bsz * seq, D_MODEL), BF16),
        grid=(bsz, D_MODEL // LANES),
        in_specs=[blk(C_FQ), blk(C_FK), blk(C_FV), blk(C_FG),
                  pl.BlockSpec((None, None, 2, seq), lambda b, h: (b, h, 0, 0))],
        out_specs=pl.BlockSpec((seq, LANES), lambda b, h: (b, h)),
        scratch_shapes=[pltpu.VMEM((2, LANES, seq), BF16), pltpu.VMEM((2, seq, LANES), BF16),
                        pltpu.VMEM((2, 2, tile, seq), F32)],
        compiler_params=pltpu.CompilerParams(
            dimension_semantics=("arbitrary", "arbitrary"), vmem_limit_bytes=VMEM_LIMIT),
        name="fox",
    )(p, p, p, p, ft4)


def _diff_kernel(q_ref, k_ref, v_ref, g_ref, cos_ref, s1_ref, s2_ref, lam_ref, sub_ref,
                 o_ref, kt_scr, s_scr, *, seq, tile, lam_init):
    def rope(x, r0):
        return (x * cos_ref[r0:r0 + tile, :]
                + pltpu.roll(x, LANES - ROT_HALF, 1) * s1_ref[r0:r0 + tile, :]
                + pltpu.roll(x, ROT_HALF, 1) * s2_ref[r0:r0 + tile, :])

    for r0 in range(0, seq, tile):
        kt_scr[:, r0:r0 + tile] = rope(k_ref[r0:r0 + tile, :].astype(F32), r0).T.astype(BF16)

    lp = lam_ref[...]
    lam = (jnp.exp(jnp.sum(lp[0:1] * lp[1:2], axis=1, keepdims=True))
           - jnp.exp(jnp.sum(lp[2:3] * lp[3:4], axis=1, keepdims=True)) + lam_init)

    lane = lax.broadcasted_iota(jnp.int32, (1, LANES), 1)
    first = lane < HEAD_DIM
    row = lax.broadcasted_iota(jnp.int32, (tile, tile), 0)
    col = lax.broadcasted_iota(jnp.int32, (tile, tile), 1)
    chunk_causal = (col // MASK_CHUNK) <= (row // MASK_CHUNK)

    def queries(i):
        r0 = i * tile
        q = rope(q_ref[r0:r0 + tile, :].astype(F32), r0).astype(BF16)
        return [jnp.where(first, q, jnp.zeros_like(q)), jnp.where(first, jnp.zeros_like(q), q)]

    def finish(i, accs, sums):
        r0 = i * tile
        o = accs[0] / sums[0] - lam * (accs[1] / sums[1])
        ms = jnp.mean(o * o, axis=-1, keepdims=True)
        o = o * lax.rsqrt(ms + EPS) * sub_ref[...] * (1.0 - lam_init)
        g = g_ref[r0:r0 + tile, :].astype(F32)
        o_ref[r0:r0 + tile, :] = (o * (g * _sigmoid(g))).astype(BF16)

    _two_pass_sweep(seq // tile, queries, ([kt_scr, kt_scr], s_scr, tile, chunk_causal),
                    ([v_ref, v_ref], True), finish)


def _diff(p, cos_t, s1_t, s2_t, lam_p, sub_w, bsz, seq, lam_init):
    tile = min(ATT_TILE, seq)
    blk = lambda off: pl.BlockSpec((seq, LANES), lambda b, h, off=off: (b, off // LANES + h))
    const = lambda shape: pl.BlockSpec(shape, lambda b, h: (0, 0))
    return pl.pallas_call(
        functools.partial(_diff_kernel, seq=seq, tile=tile, lam_init=lam_init),
        out_shape=jax.ShapeDtypeStruct((bsz * seq, D_MODEL), BF16),
        grid=(bsz, DIFF_HEADS),
        in_specs=[blk(C_DQ), blk(C_DK), blk(C_DV), blk(C_DG),
                  const((seq, LANES)), const((seq, LANES)), const((seq, LANES)),
                  const((4, HEAD_DIM)), const((1, LANES))],
        out_specs=pl.BlockSpec((seq, LANES), lambda b, h: (b, h)),
        scratch_shapes=[pltpu.VMEM((LANES, seq), BF16), pltpu.VMEM((2, 2, tile, seq), F32)],
        compiler_params=pltpu.CompilerParams(
            dimension_semantics=("arbitrary", "arbitrary"), vmem_limit_bytes=VMEM_LIMIT),
        name="diffattn",
    )(p, p, p, p, cos_t, s1_t, s2_t, lam_p, sub_w)


def _ssd_kernel(xbc_ref, z_ref, sm_ref, cw_ref, cb_ref, dtb_ref, alog_ref, dskip_ref, nw_ref,
                e_ref, shift_ref, tril_ref, o_ref, u_scr, st_scr, *, q, nsub):
    @pl.when(pl.program_id(1) == 0)
    def _():
        u_scr[0:CONV_PAD, :] = jnp.zeros((CONV_PAD, 2 * D_MODEL), BF16)
        st_scr[...] = jnp.zeros_like(st_scr)

    rows = q * nsub
    u_scr[CONV_PAD:CONV_PAD + rows, :] = xbc_ref[...]
    lane = lax.broadcasted_iota(jnp.int32, (1, LANES), 1)
    is_head = jnp.logical_and(lane >= N_SMALL, lane < 2 * N_SMALL)
    a_neg = jnp.where(is_head, -jnp.exp(alog_ref[...]), 0.0)
    row = lax.broadcasted_iota(jnp.int32, (q, q), 0)
    col = lax.broadcasted_iota(jnp.int32, (q, q), 1)
    tril = col <= row
    first = lane < HEAD_DIM

    for sc in range(nsub):
        r0 = sc * q
        delayed = _dot(shift_ref[...], u_scr[r0:r0 + CONV_PAD + q, :])
        conv = cb_ref[...] + cw_ref[CONV_K - 1:CONV_K, :] * xbc_ref[r0:r0 + q, :].astype(F32)
        for kk in range(CONV_K - 1):
            conv = conv + cw_ref[kk:kk + 1, :] * delayed[kk * q:(kk + 1) * q, :]
        act = conv * _sigmoid(conv)
        xs = act[:, 0:D_MODEL]

        dt = _softplus(sm_ref[r0:r0 + q, :] + dtb_ref[...])
        a_cs = _exact_dot_l(tril_ref[...], dt * a_neg)
        ea = jnp.exp(a_cs)
        eds = jnp.exp(a_cs[q - 1:q, :] - a_cs)
        stacked = jnp.concatenate([dt, ea, eds], axis=0)
        hi = stacked.astype(BF16)
        mid = (stacked - hi.astype(F32)).astype(BF16)
        wide = _dot(hi, e_ref[...]) + _dot(mid, e_ref[...])
        dt_x, ea_x, eds_x = wide[0:q], wide[q:2 * q], wide[2 * q:3 * q]
        a_cs_t = a_cs.T

        xdt_f = xs * dt_x
        xdt = xdt_f.astype(BF16)
        xdt_s = (xdt_f * eds_x).astype(BF16)

        ys = []
        for g in range(SSM_GROUPS):
            lo = g * SSM_GROUP_W
            bg = act[:, D_MODEL + g * SSM_STATE:D_MODEL + (g + 1) * SSM_STATE].astype(BF16)
            cg = act[:, D_MODEL + SSM_GROUPS * SSM_STATE + g * SSM_STATE:
                     D_MODEL + SSM_GROUPS * SSM_STATE + (g + 1) * SSM_STATE].astype(BF16)
            cb = _dot_nt(cg, bg)
            state = st_scr[g]
            y_off = _dot(cg, state.astype(BF16)) * ea_x[:, lo:lo + SSM_GROUP_W]
            y_diag = []
            for pp in range(2):
                xp = xdt[:, lo + pp * LANES:lo + (pp + 1) * LANES]
                halves = []
                for hh in range(2):
                    hl = N_SMALL + g * 4 + pp * 2 + hh
                    seg = a_cs[:, hl:hl + 1] - a_cs_t[hl:hl + 1, :]
                    decay = jnp.where(tril, jnp.exp(seg), 0.0)
                    halves.append(_dot((cb * decay).astype(BF16), xp))
                y_diag.append(jnp.where(first, halves[0], halves[1]))
            new_states = lax.dot_general(bg, xdt_s[:, lo:lo + SSM_GROUP_W],
                                         (((0,), (0,)), ((), ())), preferred_element_type=F32)
            st_scr[g] = state * ea_x[q - 1:q, lo:lo + SSM_GROUP_W] + new_states
            ys.append(jnp.concatenate(y_diag, axis=1) + y_off)

        y = jnp.concatenate(ys, axis=1) + xs * dskip_ref[...]
        z = z_ref[r0:r0 + q, :].astype(F32)
        yg = y * (z * _sigmoid(z))
        outs = []
        for g in range(SSM_GROUPS):
            v = yg[:, g * SSM_GROUP_W:(g + 1) * SSM_GROUP_W]
            ms = jnp.mean(v * v, axis=-1, keepdims=True)
            outs.append(v * lax.rsqrt(ms + EPS))
        o_ref[r0:r0 + q, :] = (jnp.concatenate(outs, axis=1) * nw_ref[...]).astype(BF16)

    u_scr[CONV_PAD - CONV_HALO:CONV_PAD, :] = xbc_ref[rows - CONV_HALO:rows, :]


def _conv_shift_matrix(q):
    m = np.zeros((3 * q, CONV_PAD + q), np.float32)
    for k in range(CONV_K - 1):
        t = np.arange(q)
        m[k * q + t, CONV_PAD + t - (CONV_K - 1) + k] = 1.0
    return jnp.asarray(m, BF16)


def _ssd(p, small, conv_w, conv_b, dtb_pad, alog_pad, dskip_x, norm_w, expand, bsz, seq):
    q = min(SSD_CHUNK, seq)
    nsub = SSD_SUBCHUNKS if seq % (q * SSD_SUBCHUNKS) == 0 else 1
    rows = q * nsub
    nc = seq // rows
    const = lambda shape: pl.BlockSpec(shape, lambda b, c: (0, 0))
    return pl.pallas_call(
        functools.partial(_ssd_kernel, q=q, nsub=nsub),
        out_shape=jax.ShapeDtypeStruct((bsz * seq, D_MODEL), BF16),
        grid=(bsz, nc),
        in_specs=[
            pl.BlockSpec((rows, 2 * D_MODEL), lambda b, c: (b * nc + c, C_XBC // (2 * D_MODEL))),
            pl.BlockSpec((rows, D_MODEL), lambda b, c: (b * nc + c, C_SZ // D_MODEL)),
            pl.BlockSpec((rows, LANES), lambda b, c: (b * nc + c, 0)),
            const((CONV_K, 2 * D_MODEL)), const((1, 2 * D_MODEL)),
            const((1, LANES)), const((1, LANES)),
            const((1, D_MODEL)), const((1, D_MODEL)),
            const((LANES, D_MODEL)), const((3 * q, CONV_PAD + q)), const((q, q)),
        ],
        out_specs=pl.BlockSpec((rows, D_MODEL), lambda b, c: (b * nc + c, 0)),
        scratch_shapes=[pltpu.VMEM((CONV_PAD + rows, 2 * D_MODEL), BF16),
                        pltpu.VMEM((SSM_GROUPS, SSM_STATE, SSM_GROUP_W), F32)],
        compiler_params=pltpu.CompilerParams(
            dimension_semantics=("arbitrary", "arbitrary"), vmem_limit_bytes=VMEM_LIMIT),
        name="ssd",
    )(p, p, small, conv_w, conv_b, dtb_pad, alog_pad, dskip_x, norm_w, expand,
      _conv_shift_matrix(q), jnp.asarray(np.tril(np.ones((q, q), np.float32)), BF16))


def _merge_kernel(x_ref, ya_ref, yb_ref, yc_ref, g0_ref, g1_ref, g2_ref, wb_ref, wo_ref,
                  fw_ref, o_ref, *, final):
    merged = None
    for n, (y_ref, g_ref) in enumerate(((ya_ref, g0_ref), (yb_ref, g1_ref), (yc_ref, g2_ref))):
        term = _sigmoid(g_ref[...].astype(F32)) * _dot(y_ref[...], wb_ref[n])
        merged = term if merged is None else merged + term
    out = x_ref[...] + _dot(merged.astype(BF16), wo_ref[...])
    if final:
        ms = jnp.mean(out * out, axis=-1, keepdims=True)
        out = out * lax.rsqrt(ms + EPS) * fw_ref[...]
    o_ref[...] = out


def _merge(x2, y_a, y_b, y_c, p, w_branch, w_out, final_w, final):
    t = x2.shape[0]
    tm = min(MERGE_TM, t)
    rows = lambda: pl.BlockSpec((tm, D_MODEL), lambda i: (i, 0))
    gate = lambda n: pl.BlockSpec((tm, D_MODEL), lambda i, n=n: (i, C_MG // D_MODEL + n))
    return pl.pallas_call(
        functools.partial(_merge_kernel, final=final),
        out_shape=jax.ShapeDtypeStruct((t, D_MODEL), F32),
        grid=(t // tm,),
        in_specs=[rows(), rows(), rows(), rows(), gate(0), gate(1), gate(2),
                  pl.BlockSpec((3, D_MODEL, D_MODEL), lambda i: (0, 0, 0)),
                  pl.BlockSpec((D_MODEL, D_MODEL), lambda i: (0, 0)),
                  pl.BlockSpec((1, D_MODEL), lambda i: (0, 0))],
        out_specs=rows(),
        compiler_params=pltpu.CompilerParams(
            dimension_semantics=("arbitrary",), vmem_limit_bytes=VMEM_LIMIT),
        name="merge",
    )(x2, y_a, y_b, y_c, p, p, p, w_branch, w_out, final_w)


def _rope_tables(seq):
    pos = jnp.arange(seq, dtype=F32)
    inv_freq = ROPE_THETA ** (-jnp.arange(0, 2 * ROT_HALF, 2, dtype=F32) / (2 * ROT_HALF))
    ang = pos[:, None] * inv_freq[None, :]
    cos, sin = jnp.cos(ang), jnp.sin(ang)
    ones = jnp.ones((seq, HEAD_DIM - 2 * ROT_HALF), F32)
    zeros = jnp.zeros((seq, ROT_HALF), F32)
    rest = jnp.zeros((seq, HEAD_DIM - 2 * ROT_HALF), F32)
    cos_c = jnp.concatenate([cos, cos, ones], axis=1)
    s1_c = jnp.concatenate([-sin, zeros, rest], axis=1)
    s2_c = jnp.concatenate([zeros, sin, rest], axis=1)
    two = lambda t: jnp.concatenate([t, t], axis=1)
    return two(cos_c), two(s1_c), two(s2_c)


def _pad_lanes(v, start):
    return jnp.zeros((1, LANES), F32).at[0, start:start + v.shape[0]].set(v.astype(F32))


def _layer_weights(w_in):
    o = np.cumsum([0, 1024, 1024, 1024, 16, 1024, 1024, 2048, 16, 1024, 1024, 1024, 1024, 3072])
    seg = lambda i: w_in[:, o[i]:o[i + 1]]
    scale = LOG2E * HEAD_DIM ** -0.5
    w_main = jnp.concatenate(
        [seg(0) * scale, seg(1), seg(2), seg(4), seg(6), seg(5),
         seg(8) * scale, seg(9), seg(10), seg(11), seg(12)], axis=1).astype(BF16)
    w_small = jnp.concatenate(
        [seg(3), seg(7), jnp.zeros((D_MODEL, LANES - 2 * N_SMALL), F32)], axis=1).astype(BF16)
    return w_main, w_small


def kernel(x, norm_w, w_in, b_forget, conv_w, conv_b, dt_bias, a_log, d_skip, ssm_norm_w,
           diff_lambda, subln_w, w_branch, w_out, final_norm_w):
    bsz, seq, _ = x.shape
    depth = norm_w.shape[0]
    cos_t, s1_t, s2_t = _rope_tables(seq)
    heads = jnp.arange(D_MODEL) // HEAD_DIM
    expand = (jnp.arange(LANES)[:, None] == (heads[None, :] + N_SMALL)).astype(BF16)

    x2 = x.reshape(bsz * seq, D_MODEL)
    for layer in range(depth):
        w_main, w_small = _layer_weights(w_in[layer])
        p, small = _inproj(x2, norm_w[layer][None, :], w_main, w_small)

        ft = _fprep(small, _pad_lanes(b_forget[layer], 0), bsz, seq)
        y_a = _fox(p, ft.reshape(bsz, N_SMALL // 2, 2, seq), bsz, seq)

        y_b = _ssd(p, small, conv_w[layer], conv_b[layer][None, :],
                   _pad_lanes(dt_bias[layer], N_SMALL),
                   _pad_lanes(a_log[layer], N_SMALL),
                   jnp.repeat(d_skip[layer], HEAD_DIM)[None, :],
                   ssm_norm_w[layer][None, :], expand, bsz, seq)

        lam_init = 0.8 - 0.6 * math.exp(-0.3 * layer)
        y_c = _diff(p, cos_t, s1_t, s2_t, diff_lambda[layer], subln_w[layer][None, :],
                    bsz, seq, lam_init)

        x2 = _merge(x2, y_a, y_b, y_c, p, w_branch[layer].astype(BF16),
                    w_out[layer].astype(BF16), final_norm_w[None, :], layer == depth - 1)
    return x2.reshape(bsz, seq, D_MODEL)
```

```python
import functools
import math

import numpy as np
import jax
import jax.numpy as jnp
from jax import lax
from jax.experimental import pallas as pl
from jax.experimental.pallas import tpu as pltpu

F32 = jnp.float32
BF16 = jnp.bfloat16

D_MODEL = 1024
EPS = 1e-6
LOG2E = 1.4426950408889634
HEAD_DIM = 64
LANES = 128
N_SMALL = 16
SSM_GROUPS = 4
SSM_STATE = 128
SSM_GROUP_W = 256
CONV_K = 4
CONV_HALO = 16
CONV_PAD = 128
DIFF_HEADS = 8
ROT_HALF = 8
ROPE_THETA = 500000.0
MASK_CHUNK = 64
SUM_ROWS = 16

C_FQ, C_FK, C_FV, C_FG = 0, 1024, 2048, 3072
C_XBC, C_SZ = 4096, 6144
C_DQ, C_DK, C_DV, C_DG = 7168, 8192, 9216, 10240
C_MG = 11264
N_MAIN = 14336

ATT_TILE = 256
SSD_CHUNK = 128
SSD_SUBCHUNKS = 4
PROJ_TM, PROJ_TN = 2048, 1024
MERGE_TM = 512
VMEM_LIMIT = 56 * 1024 * 1024


def _sigmoid(x):
    return 0.5 * jnp.tanh(0.5 * x) + 0.5


def _silu(x):
    h = 0.5 * x
    return h + h * jnp.tanh(h)


def _softplus(x):
    return jnp.maximum(x, 0.0) + jnp.log1p(jnp.exp(-jnp.abs(x)))


def _split3(x):
    hi = x.astype(BF16)
    r1 = x - hi.astype(F32)
    mid = r1.astype(BF16)
    lo = (r1 - mid.astype(F32)).astype(BF16)
    return hi, mid, lo


def _dot(a, b):
    return jnp.dot(a, b, preferred_element_type=F32)


def _dot_nt(a, b):
    return lax.dot_general(a, b, (((1,), (1,)), ((), ())), preferred_element_type=F32)


def _exact_dot_l(m01, x):
    hi, mid, lo = _split3(x)
    return _dot(m01, hi) + _dot(m01, mid) + _dot(m01, lo)


def _tril_bf16(n):
    r = lax.broadcasted_iota(jnp.int32, (n, n), 0)
    c = lax.broadcasted_iota(jnp.int32, (n, n), 1)
    return jnp.where(r >= c, 1.0, 0.0).astype(BF16)


def _inproj_kernel(x_ref, nw_ref, w_ref, ws_ref, p_ref, s_ref, h_ref):
    @pl.when(pl.program_id(1) == 0)
    def _():
        x = x_ref[...]
        ms = jnp.mean(x * x, axis=-1, keepdims=True)
        h = (x * lax.rsqrt(ms + EPS) * nw_ref[...]).astype(BF16)
        h_ref[...] = h
        s_ref[...] = _dot(h, ws_ref[...])

    p_ref[...] = _dot(h_ref[...], w_ref[...]).astype(BF16)


def _inproj(x2, norm_w, w_main, w_small):
    t = x2.shape[0]
    tm, tn = min(PROJ_TM, t), PROJ_TN
    return pl.pallas_call(
        _inproj_kernel,
        out_shape=(jax.ShapeDtypeStruct((t, N_MAIN), BF16),
                   jax.ShapeDtypeStruct((t, LANES), F32)),
        grid=(t // tm, N_MAIN // tn),
        in_specs=[
            pl.BlockSpec((tm, D_MODEL), lambda i, j: (i, 0)),
            pl.BlockSpec((1, D_MODEL), lambda i, j: (0, 0)),
            pl.BlockSpec((D_MODEL, tn), lambda i, j: (0, j)),
            pl.BlockSpec((D_MODEL, LANES), lambda i, j: (0, 0)),
        ],
        out_specs=(pl.BlockSpec((tm, tn), lambda i, j: (i, j)),
                   pl.BlockSpec((tm, LANES), lambda i, j: (i, 0))),
        scratch_shapes=[pltpu.VMEM((tm, D_MODEL), BF16)],
        compiler_params=pltpu.CompilerParams(
            dimension_semantics=("arbitrary", "arbitrary"), vmem_limit_bytes=VMEM_LIMIT),
        name="inproj",
    )(x2, norm_w, w_main, w_small)


def _fprep_kernel(s_ref, bias_ref, ft_ref, f_scr, *, seq):
    tril = _tril_bf16(LANES)
    carry = jnp.zeros((1, LANES), F32)
    for blk in range(seq // LANES):
        x = s_ref[blk * LANES:(blk + 1) * LANES, :] + bias_ref[...]
        log_f = jnp.minimum(x, 0.0) - jnp.log1p(jnp.exp(-jnp.abs(x)))
        c = _exact_dot_l(tril, log_f) + carry
        f_scr[blk * LANES:(blk + 1) * LANES, :] = c
        carry = c[LANES - 1:LANES, :]
    ft_ref[...] = f_scr[...].T[0:N_SMALL, :]


def _fprep(small, bias_pad, bsz, seq):
    return pl.pallas_call(
        functools.partial(_fprep_kernel, seq=seq),
        out_shape=jax.ShapeDtypeStruct((bsz, N_SMALL, seq), F32),
        grid=(bsz,),
        in_specs=[pl.BlockSpec((seq, LANES), lambda b: (b, 0)),
                  pl.BlockSpec((1, LANES), lambda b: (0, 0))],
        out_specs=pl.BlockSpec((None, N_SMALL, seq), lambda b: (b, 0, 0)),
        scratch_shapes=[pltpu.VMEM((seq, LANES), F32)],
        compiler_params=pltpu.CompilerParams(
            dimension_semantics=("arbitrary",), vmem_limit_bytes=VMEM_LIMIT),
        name="fprep",
    )(small, bias_pad)


def _sublane_tiles(x, op):
    t = x[0:8, :]
    for r in range(1, x.shape[0] // 8):
        t = op(t, x[8 * r:8 * (r + 1), :])
    return t


def _score_block(k_refs, qts, s_refs, mruns, j, tile, mask):
    r0 = j * tile
    for h, qt in enumerate(qts):
        s = _dot(k_refs[h][r0:r0 + tile, :], qt)
        if mask is not None:
            s = jnp.where(mask, s, -jnp.inf)
        s_refs[h][r0:r0 + tile, :] = s
        t = _sublane_tiles(s, jnp.maximum)
        mruns[h] = t if mruns[h] is None else jnp.maximum(mruns[h], t)


def _weight_block(ms, s_refs, vt_refs, accs, j, tile):
    r0 = j * tile
    for h in range(len(ms)):
        p = jnp.exp2(s_refs[h][r0:r0 + tile, :] - ms[h])
        pv = _dot(vt_refs[h][:, r0:r0 + tile], p.astype(BF16))
        accs[h] = pv if accs[h] is None else accs[h] + pv


def _two_pass_sweep(n_tiles, prepare, make_queries, k_refs, vt_refs, s_scr, tile, mask, finish):
    bufs = lambda i: [s_scr[2 * (i % 2) + h] for h in range(2)]
    prepare(0)
    mruns = [None, None]
    _score_block(k_refs, make_queries(0), bufs(0), mruns, 0, tile, mask)
    ms = [jnp.max(t, axis=0, keepdims=True) for t in mruns]
    for i in range(n_tiles):
        accs, mruns, qts = [None, None], [None, None], None
        if i + 1 < n_tiles:
            prepare(i + 1)
            qts = make_queries(i + 1)
            for j in range(i + 2):
                _score_block(k_refs, qts, bufs(i + 1), mruns, j, tile, mask if j == i + 1 else None)
        for j in range(i + 1):
            _weight_block(ms, bufs(i), vt_refs, accs, j, tile)
        finish(i, accs)
        if qts is not None:
            ms = [jnp.max(t, axis=0, keepdims=True) for t in mruns]


def _fox_kernel(q_ref, k_ref, v_ref, g_ref, ft_ref, o_ref, k_scr, vt_scr, *s_scr, seq, tile):
    lane = lax.broadcasted_iota(jnp.int32, (1, LANES), 1)
    sub = lax.broadcasted_iota(jnp.int32, (LANES, 1), 0)
    key = lax.broadcasted_iota(jnp.int32, (tile, tile), 0)
    qry = lax.broadcasted_iota(jnp.int32, (tile, tile), 1)
    causal = key <= qry
    own = (lane < HEAD_DIM, lane >= HEAD_DIM)
    own_t = (sub < HEAD_DIM, sub >= HEAD_DIM)
    bias_lo = (HEAD_DIM, 0)
    ones_q = tuple(jnp.logical_and(lane >= b, lane < b + 3).astype(F32) for b in bias_lo)

    def prepare(i):
        r0 = i * tile
        k = k_ref[r0:r0 + tile, :]
        vt = v_ref[r0:r0 + tile, :].astype(F32).T
        for hh in range(2):
            b = bias_lo[hh]
            hi, mid, lo = _split3(ft_ref[hh:hh + 1, r0:r0 + tile] * (-LOG2E))
            rows = jnp.concatenate([hi.astype(F32), mid.astype(F32), lo.astype(F32),
                                    jnp.zeros((5, tile), F32)], axis=0)
            block = jnp.concatenate(([jnp.zeros((b, tile), F32)] if b else []) + [rows]
                                    + [jnp.zeros((LANES - b - 8, tile), F32)], axis=0)
            k_scr[hh, r0:r0 + tile, :] = jnp.where(own[hh], k, block.T.astype(BF16))
            vt_scr[hh, :, r0:r0 + tile] = jnp.where(own_t[hh], vt, 1.0).astype(BF16)

    def queries(i):
        q = q_ref[i * tile:(i + 1) * tile, :].astype(F32)
        return [jnp.where(own[hh], q, ones_q[hh]).T.astype(BF16) for hh in range(2)]

    def finish(i, accs):
        r0 = i * tile
        num = jnp.where(own_t[0], accs[0], accs[1])
        den = jnp.where(own_t[0], accs[0][HEAD_DIM:HEAD_DIM + 1, :], accs[1][0:1, :])
        g = g_ref[r0:r0 + tile, :].astype(F32)
        o_ref[r0:r0 + tile, :] = ((num / den).T * _silu(g)).astype(BF16)

    _two_pass_sweep(seq // tile, prepare, queries, [k_scr.at[0], k_scr.at[1]],
                    [vt_scr.at[0], vt_scr.at[1]], s_scr, tile, causal, finish)


def _fox(p, ft4, bsz, seq):
    tile = min(ATT_TILE, seq)
    blk = lambda off: pl.BlockSpec((seq, LANES), lambda b, h, off=off: (b, off // LANES + h))
    return pl.pallas_call(
        functools.partial(_fox_kernel, seq=seq, tile=tile),
        out_shape=jax.ShapeDtypeStruct((bsz * seq, D_MODEL), BF16),
        grid=(bsz, D_MODEL // LANES),
        in_specs=[blk(C_FQ), blk(C_FK), blk(C_FV), blk(C_FG),
                  pl.BlockSpec((None, None, 2, seq), lambda b, h: (b, h, 0, 0))],
        out_specs=pl.BlockSpec((seq, LANES), lambda b, h: (b, h)),
        scratch_shapes=[pltpu.VMEM((2, seq, LANES), BF16), pltpu.VMEM((2, LANES, seq), BF16)]
        + [pltpu.VMEM((seq, tile), F32)] * 4,
        compiler_params=pltpu.CompilerParams(
            dimension_semantics=("arbitrary", "arbitrary"), vmem_limit_bytes=VMEM_LIMIT),
        name="fox",
    )(p, p, p, p, ft4)


def _diff_kernel(q_ref, k_ref, v_ref, g_ref, cos_ref, s1_ref, s2_ref, lam_ref, sub_ref,
                 o_ref, k_scr, vt_scr, *s_scr, seq, tile, lam_init):
    def rope(x, r0):
        return (x * cos_ref[r0:r0 + tile, :]
                + pltpu.roll(x, LANES - ROT_HALF, 1) * s1_ref[r0:r0 + tile, :]
                + pltpu.roll(x, ROT_HALF, 1) * s2_ref[r0:r0 + tile, :])

    def prepare(i):
        r0 = i * tile
        k_scr[r0:r0 + tile, :] = rope(k_ref[r0:r0 + tile, :].astype(F32), r0).astype(BF16)
        vt = v_ref[r0:r0 + tile, :].astype(F32).T
        vt_scr[:, r0:r0 + tile] = jnp.concatenate(
            [vt, jnp.ones((SUM_ROWS, tile), F32)], axis=0).astype(BF16)

    lp = lam_ref[...]
    lam = (jnp.exp(jnp.sum(lp[0:1] * lp[1:2], axis=1, keepdims=True))
           - jnp.exp(jnp.sum(lp[2:3] * lp[3:4], axis=1, keepdims=True)) + lam_init)

    lane = lax.broadcasted_iota(jnp.int32, (1, LANES), 1)
    first = lane < HEAD_DIM
    key = lax.broadcasted_iota(jnp.int32, (tile, tile), 0)
    qry = lax.broadcasted_iota(jnp.int32, (tile, tile), 1)
    chunk_causal = (key // MASK_CHUNK) <= (qry // MASK_CHUNK)

    def queries(i):
        r0 = i * tile
        q = rope(q_ref[r0:r0 + tile, :].astype(F32), r0)
        return [jnp.where(first, q, 0.0).T.astype(BF16), jnp.where(first, 0.0, q).T.astype(BF16)]

    def finish(i, accs):
        r0 = i * tile
        o = (accs[0][0:LANES] / accs[0][LANES:LANES + 1]
             - lam * (accs[1][0:LANES] / accs[1][LANES:LANES + 1])).T
        ms = jnp.mean(o * o, axis=-1, keepdims=True)
        o = o * lax.rsqrt(ms + EPS) * sub_ref[...] * (1.0 - lam_init)
        g = g_ref[r0:r0 + tile, :].astype(F32)
        o_ref[r0:r0 + tile, :] = (o * _silu(g)).astype(BF16)

    _two_pass_sweep(seq // tile, prepare, queries, [k_scr, k_scr], [vt_scr, vt_scr],
                    s_scr, tile, chunk_causal, finish)


def _diff(p, cos_t, s1_t, s2_t, lam_p, sub_w, bsz, seq, lam_init):
    tile = min(ATT_TILE, seq)
    blk = lambda off: pl.BlockSpec((seq, LANES), lambda b, h, off=off: (b, off // LANES + h))
    const = lambda shape: pl.BlockSpec(shape, lambda b, h: (0, 0))
    return pl.pallas_call(
        functools.partial(_diff_kernel, seq=seq, tile=tile, lam_init=lam_init),
        out_shape=jax.ShapeDtypeStruct((bsz * seq, D_MODEL), BF16),
        grid=(bsz, DIFF_HEADS),
        in_specs=[blk(C_DQ), blk(C_DK), blk(C_DV), blk(C_DG),
                  const((seq, LANES)), const((seq, LANES)), const((seq, LANES)),
                  const((4, HEAD_DIM)), const((1, LANES))],
        out_specs=pl.BlockSpec((seq, LANES), lambda b, h: (b, h)),
        scratch_shapes=[pltpu.VMEM((seq, LANES), BF16), pltpu.VMEM((LANES + SUM_ROWS, seq), BF16)]
        + [pltpu.VMEM((seq, tile), F32)] * 4,
        compiler_params=pltpu.CompilerParams(
            dimension_semantics=("arbitrary", "arbitrary"), vmem_limit_bytes=VMEM_LIMIT),
        name="diffattn",
    )(p, p, p, p, cos_t, s1_t, s2_t, lam_p, sub_w)


def _ssd_kernel(xbc_ref, z_ref, sm_ref, cw_ref, cb_ref, dtb_ref, alog_ref, dskip_ref, nw_ref,
                e_ref, shift_ref, tril_ref, o_ref, u_scr, st_scr, *, q, nsub):
    @pl.when(pl.program_id(1) == 0)
    def _():
        u_scr[0:CONV_PAD, :] = jnp.zeros((CONV_PAD, 2 * D_MODEL), BF16)
        st_scr[...] = jnp.zeros_like(st_scr)

    rows = q * nsub
    u_scr[CONV_PAD:CONV_PAD + rows, :] = xbc_ref[...]
    lane = lax.broadcasted_iota(jnp.int32, (1, LANES), 1)
    is_head = jnp.logical_and(lane >= N_SMALL, lane < 2 * N_SMALL)
    a_neg = jnp.where(is_head, -jnp.exp(alog_ref[...]), 0.0)
    row = lax.broadcasted_iota(jnp.int32, (q, q), 0)
    col = lax.broadcasted_iota(jnp.int32, (q, q), 1)
    tril = col <= row
    first = lane < HEAD_DIM

    for sc in range(nsub):
        r0 = sc * q
        delayed = _dot(shift_ref[...], u_scr[r0:r0 + CONV_PAD + q, :])
        conv = cb_ref[...] + cw_ref[CONV_K - 1:CONV_K, :] * xbc_ref[r0:r0 + q, :].astype(F32)
        for kk in range(CONV_K - 1):
            conv = conv + cw_ref[kk:kk + 1, :] * delayed[kk * q:(kk + 1) * q, :]
        act = _silu(conv)
        xs = act[:, 0:D_MODEL]

        dt = _softplus(sm_ref[r0:r0 + q, :] + dtb_ref[...])
        a_cs = _exact_dot_l(tril_ref[...], dt * a_neg)
        ea = jnp.exp(a_cs)
        eds = jnp.exp(a_cs[q - 1:q, :] - a_cs)
        stacked = jnp.concatenate([ea, eds * dt], axis=0)
        hi = stacked.astype(BF16)
        mid = (stacked - hi.astype(F32)).astype(BF16)
        wide = _dot(hi, e_ref[...]) + _dot(mid, e_ref[...])
        ea_x, w_x = wide[0:q], wide[q:2 * q]
        a_cs_t = a_cs.T
        dt_t = dt.T

        xs_b = xs.astype(BF16)
        xdt_s = (xs * w_x).astype(BF16)

        ys = []
        for g in range(SSM_GROUPS):
            lo = g * SSM_GROUP_W
            bg = act[:, D_MODEL + g * SSM_STATE:D_MODEL + (g + 1) * SSM_STATE].astype(BF16)
            cg = act[:, D_MODEL + SSM_GROUPS * SSM_STATE + g * SSM_STATE:
                     D_MODEL + SSM_GROUPS * SSM_STATE + (g + 1) * SSM_STATE].astype(BF16)
            cb = _dot_nt(cg, bg)
            state = st_scr[g]
            y_off = _dot(cg, state.astype(BF16)) * ea_x[:, lo:lo + SSM_GROUP_W]
            y_diag = []
            for pp in range(2):
                xp = xs_b[:, lo + pp * LANES:lo + (pp + 1) * LANES]
                halves = []
                for hh in range(2):
                    hl = N_SMALL + g * 4 + pp * 2 + hh
                    seg = a_cs[:, hl:hl + 1] - a_cs_t[hl:hl + 1, :]
                    decay = jnp.where(tril, jnp.exp(seg), 0.0) * dt_t[hl:hl + 1, :]
                    halves.append(_dot((cb * decay).astype(BF16), xp))
                y_diag.append(jnp.where(first, halves[0], halves[1]))
            new_states = lax.dot_general(bg, xdt_s[:, lo:lo + SSM_GROUP_W],
                                         (((0,), (0,)), ((), ())), preferred_element_type=F32)
            st_scr[g] = state * ea_x[q - 1:q, lo:lo + SSM_GROUP_W] + new_states
            ys.append(jnp.concatenate(y_diag, axis=1) + y_off)

        y = jnp.concatenate(ys, axis=1) + xs * dskip_ref[...]
        z = z_ref[r0:r0 + q, :].astype(F32)
        yg = y * _silu(z)
        outs = []
        for g in range(SSM_GROUPS):
            v = yg[:, g * SSM_GROUP_W:(g + 1) * SSM_GROUP_W]
            ms = jnp.mean(v * v, axis=-1, keepdims=True)
            outs.append(v * lax.rsqrt(ms + EPS))
        o_ref[r0:r0 + q, :] = (jnp.concatenate(outs, axis=1) * nw_ref[...]).astype(BF16)

    u_scr[CONV_PAD - CONV_HALO:CONV_PAD, :] = xbc_ref[rows - CONV_HALO:rows, :]


def _conv_shift_matrix(q):
    m = np.zeros((3 * q, CONV_PAD + q), np.float32)
    for k in range(CONV_K - 1):
        t = np.arange(q)
        m[k * q + t, CONV_PAD + t - (CONV_K - 1) + k] = 1.0
    return jnp.asarray(m, BF16)


def _ssd(p, small, conv_w, conv_b, dtb_pad, alog_pad, dskip_x, norm_w, expand, bsz, seq):
    q = min(SSD_CHUNK, seq)
    nsub = SSD_SUBCHUNKS if seq % (q * SSD_SUBCHUNKS) == 0 else 1
    rows = q * nsub
    nc = seq // rows
    const = lambda shape: pl.BlockSpec(shape, lambda b, c: (0, 0))
    return pl.pallas_call(
        functools.partial(_ssd_kernel, q=q, nsub=nsub),
        out_shape=jax.ShapeDtypeStruct((bsz * seq, D_MODEL), BF16),
        grid=(bsz, nc),
        in_specs=[
            pl.BlockSpec((rows, 2 * D_MODEL), lambda b, c: (b * nc + c, C_XBC // (2 * D_MODEL))),
            pl.BlockSpec((rows, D_MODEL), lambda b, c: (b * nc + c, C_SZ // D_MODEL)),
            pl.BlockSpec((rows, LANES), lambda b, c: (b * nc + c, 0)),
            const((CONV_K, 2 * D_MODEL)), const((1, 2 * D_MODEL)),
            const((1, LANES)), const((1, LANES)),
            const((1, D_MODEL)), const((1, D_MODEL)),
            const((LANES, D_MODEL)), const((3 * q, CONV_PAD + q)), const((q, q)),
        ],
        out_specs=pl.BlockSpec((rows, D_MODEL), lambda b, c: (b * nc + c, 0)),
        scratch_shapes=[pltpu.VMEM((CONV_PAD + rows, 2 * D_MODEL), BF16),
                        pltpu.VMEM((SSM_GROUPS, SSM_STATE, SSM_GROUP_W), F32)],
        compiler_params=pltpu.CompilerParams(
            dimension_semantics=("arbitrary", "arbitrary"), vmem_limit_bytes=VMEM_LIMIT),
        name="ssd",
    )(p, p, small, conv_w, conv_b, dtb_pad, alog_pad, dskip_x, norm_w, expand,
      _conv_shift_matrix(q), jnp.asarray(np.tril(np.ones((q, q), np.float32)), BF16))


def _merge_kernel(x_ref, ya_ref, yb_ref, yc_ref, g0_ref, g1_ref, g2_ref, wb_ref, wo_ref,
                  fw_ref, o_ref, *, final):
    merged = None
    for n, (y_ref, g_ref) in enumerate(((ya_ref, g0_ref), (yb_ref, g1_ref), (yc_ref, g2_ref))):
        term = _sigmoid(g_ref[...].astype(F32)) * _dot(y_ref[...], wb_ref[n])
        merged = term if merged is None else merged + term
    out = x_ref[...] + _dot(merged.astype(BF16), wo_ref[...])
    if final:
        ms = jnp.mean(out * out, axis=-1, keepdims=True)
        out = out * lax.rsqrt(ms + EPS) * fw_ref[...]
    o_ref[...] = out


def _merge(x2, y_a, y_b, y_c, p, w_branch, w_out, final_w, final):
    t = x2.shape[0]
    tm = min(MERGE_TM, t)
    rows = lambda: pl.BlockSpec((tm, D_MODEL), lambda i: (i, 0))
    gate = lambda n: pl.BlockSpec((tm, D_MODEL), lambda i, n=n: (i, C_MG // D_MODEL + n))
    return pl.pallas_call(
        functools.partial(_merge_kernel, final=final),
        out_shape=jax.ShapeDtypeStruct((t, D_MODEL), F32),
        grid=(t // tm,),
        in_specs=[rows(), rows(), rows(), rows(), gate(0), gate(1), gate(2),
                  pl.BlockSpec((3, D_MODEL, D_MODEL), lambda i: (0, 0, 0)),
                  pl.BlockSpec((D_MODEL, D_MODEL), lambda i: (0, 0)),
                  pl.BlockSpec((1, D_MODEL), lambda i: (0, 0))],
        out_specs=rows(),
        compiler_params=pltpu.CompilerParams(
            dimension_semantics=("arbitrary",), vmem_limit_bytes=VMEM_LIMIT),
        name="merge",
    )(x2, y_a, y_b, y_c, p, p, p, w_branch, w_out, final_w)


def _rope_tables(seq):
    pos = jnp.arange(seq, dtype=F32)
    inv_freq = ROPE_THETA ** (-jnp.arange(0, 2 * ROT_HALF, 2, dtype=F32) / (2 * ROT_HALF))
    ang = pos[:, None] * inv_freq[None, :]
    cos, sin = jnp.cos(ang), jnp.sin(ang)
    ones = jnp.ones((seq, HEAD_DIM - 2 * ROT_HALF), F32)
    zeros = jnp.zeros((seq, ROT_HALF), F32)
    rest = jnp.zeros((seq, HEAD_DIM - 2 * ROT_HALF), F32)
    cos_c = jnp.concatenate([cos, cos, ones], axis=1)
    s1_c = jnp.concatenate([-sin, zeros, rest], axis=1)
    s2_c = jnp.concatenate([zeros, sin, rest], axis=1)
    two = lambda t: jnp.concatenate([t, t], axis=1)
    return two(cos_c), two(s1_c), two(s2_c)


def _pad_lanes(v, start):
    return jnp.zeros((1, LANES), F32).at[0, start:start + v.shape[0]].set(v.astype(F32))


def _projection_weights(w_in):
    o = np.cumsum([0, 1024, 1024, 1024, 16, 1024, 1024, 2048, 16, 1024, 1024, 1024, 1024, 3072])
    seg = lambda i: w_in[:, :, o[i]:o[i + 1]]
    scale = LOG2E * HEAD_DIM ** -0.5
    w_main = jnp.concatenate(
        [seg(0) * scale, seg(1), seg(2), seg(4), seg(6), seg(5),
         seg(8) * scale, seg(9), seg(10), seg(11), seg(12)], axis=2).astype(BF16)
    w_small = jnp.concatenate(
        [seg(3), seg(7), jnp.zeros(w_in.shape[:2] + (LANES - 2 * N_SMALL,), F32)],
        axis=2).astype(BF16)
    return w_main, w_small


def kernel(x, norm_w, w_in, b_forget, conv_w, conv_b, dt_bias, a_log, d_skip, ssm_norm_w,
           diff_lambda, subln_w, w_branch, w_out, final_norm_w):
    bsz, seq, _ = x.shape
    depth = norm_w.shape[0]
    cos_t, s1_t, s2_t = _rope_tables(seq)
    heads = jnp.arange(D_MODEL) // HEAD_DIM
    expand = (jnp.arange(LANES)[:, None] == (heads[None, :] + N_SMALL)).astype(BF16)

    w_main, w_small = _projection_weights(w_in)
    w_branch_b, w_out_b = w_branch.astype(BF16), w_out.astype(BF16)

    x2 = x.reshape(bsz * seq, D_MODEL)
    for layer in range(depth):
        p, small = _inproj(x2, norm_w[layer][None, :], w_main[layer], w_small[layer])

        ft = _fprep(small, _pad_lanes(b_forget[layer], 0), bsz, seq)
        y_a = _fox(p, ft.reshape(bsz, N_SMALL // 2, 2, seq), bsz, seq)

        y_b = _ssd(p, small, conv_w[layer], conv_b[layer][None, :],
                   _pad_lanes(dt_bias[layer], N_SMALL),
                   _pad_lanes(a_log[layer], N_SMALL),
                   jnp.repeat(d_skip[layer], HEAD_DIM)[None, :],
                   ssm_norm_w[layer][None, :], expand, bsz, seq)

        lam_init = 0.8 - 0.6 * math.exp(-0.3 * layer)
        y_c = _diff(p, cos_t, s1_t, s2_t, diff_lambda[layer], subln_w[layer][None, :],
                    bsz, seq, lam_init)

        x2 = _merge(x2, y_a, y_b, y_c, p, w_branch_b[layer], w_out_b[layer],
                    final_norm_w[None, :], layer == depth - 1)
    return x2.reshape(bsz, seq, D_MODEL)
```

```python
import functools
import math

import numpy as np
import jax
import jax.numpy as jnp
from jax import lax
from jax.experimental import pallas as pl
from jax.experimental.pallas import tpu as pltpu

F32 = jnp.float32
BF16 = jnp.bfloat16

D_MODEL = 1024
EPS = 1e-6
LOG2E = 1.4426950408889634
HEAD_DIM = 64
LANES = 128
N_SMALL = 16
SSM_GROUPS = 4
SSM_STATE = 128
SSM_GROUP_W = 256
CONV_K = 4
CONV_HALO = 16
CONV_PAD = 128
DIFF_HEADS = 8
ROT_HALF = 8
ROPE_THETA = 500000.0
MASK_CHUNK = 64
SUM_ROWS = 16

C_FQ, C_FK, C_FV, C_FG = 0, 1024, 2048, 3072
C_XBC, C_SZ = 4096, 6144
C_DQ, C_DK, C_DV, C_DG = 7168, 8192, 9216, 10240
C_MG = 11264
N_MAIN = 14336

ATT_TILE = 256
SSD_CHUNK = 128
SSD_SUBCHUNKS = 4
PROJ_TM, PROJ_TN = 2048, 1024
MERGE_TM = 512
VMEM_LIMIT = 56 * 1024 * 1024


def _sigmoid(x):
    return 0.5 * jnp.tanh(0.5 * x) + 0.5


def _silu(x):
    h = 0.5 * x
    return h + h * jnp.tanh(h)


def _softplus(x):
    return jnp.maximum(x, 0.0) + jnp.log1p(jnp.exp(-jnp.abs(x)))


def _split3(x):
    hi = x.astype(BF16)
    r1 = x - hi.astype(F32)
    mid = r1.astype(BF16)
    lo = (r1 - mid.astype(F32)).astype(BF16)
    return hi, mid, lo


def _dot(a, b):
    return jnp.dot(a, b, preferred_element_type=F32)


def _dot_nt(a, b):
    return lax.dot_general(a, b, (((1,), (1,)), ((), ())), preferred_element_type=F32)


def _exact_dot_l(m01, x):
    hi, mid, lo = _split3(x)
    return _dot(m01, hi) + _dot(m01, mid) + _dot(m01, lo)


def _tril_bf16(n):
    r = lax.broadcasted_iota(jnp.int32, (n, n), 0)
    c = lax.broadcasted_iota(jnp.int32, (n, n), 1)
    return jnp.where(r >= c, 1.0, 0.0).astype(BF16)


def _inproj_kernel(x_ref, nw_ref, w_ref, ws_ref, p_ref, s_ref, h_ref):
    @pl.when(pl.program_id(1) == 0)
    def _():
        x = x_ref[...]
        ms = jnp.mean(x * x, axis=-1, keepdims=True)
        h = (x * lax.rsqrt(ms + EPS) * nw_ref[...]).astype(BF16)
        h_ref[...] = h
        s_ref[...] = _dot(h, ws_ref[...])

    p_ref[...] = _dot(h_ref[...], w_ref[...]).astype(BF16)


def _inproj(x2, norm_w, w_main, w_small):
    t = x2.shape[0]
    tm, tn = min(PROJ_TM, t), PROJ_TN
    return pl.pallas_call(
        _inproj_kernel,
        out_shape=(jax.ShapeDtypeStruct((t, N_MAIN), BF16),
                   jax.ShapeDtypeStruct((t, LANES), F32)),
        grid=(t // tm, N_MAIN // tn),
        in_specs=[
            pl.BlockSpec((tm, D_MODEL), lambda i, j: (i, 0)),
            pl.BlockSpec((1, D_MODEL), lambda i, j: (0, 0)),
            pl.BlockSpec((D_MODEL, tn), lambda i, j: (0, j)),
            pl.BlockSpec((D_MODEL, LANES), lambda i, j: (0, 0)),
        ],
        out_specs=(pl.BlockSpec((tm, tn), lambda i, j: (i, j)),
                   pl.BlockSpec((tm, LANES), lambda i, j: (i, 0))),
        scratch_shapes=[pltpu.VMEM((tm, D_MODEL), BF16)],
        compiler_params=pltpu.CompilerParams(
            dimension_semantics=("arbitrary", "arbitrary"), vmem_limit_bytes=VMEM_LIMIT),
        name="inproj",
    )(x2, norm_w, w_main, w_small)


def _fprep_kernel(s_ref, bias_ref, ft_ref, f_scr, *, seq):
    tril = _tril_bf16(LANES)
    carry = jnp.zeros((1, LANES), F32)
    for blk in range(seq // LANES):
        x = s_ref[blk * LANES:(blk + 1) * LANES, :] + bias_ref[...]
        log_f = jnp.minimum(x, 0.0) - jnp.log1p(jnp.exp(-jnp.abs(x)))
        c = _exact_dot_l(tril, log_f) + carry
        f_scr[blk * LANES:(blk + 1) * LANES, :] = c
        carry = c[LANES - 1:LANES, :]
    ft_ref[...] = f_scr[...].T[0:N_SMALL, :]


def _fprep(small, bias_pad, bsz, seq):
    return pl.pallas_call(
        functools.partial(_fprep_kernel, seq=seq),
        out_shape=jax.ShapeDtypeStruct((bsz, N_SMALL, seq), F32),
        grid=(bsz,),
        in_specs=[pl.BlockSpec((seq, LANES), lambda b: (b, 0)),
                  pl.BlockSpec((1, LANES), lambda b: (0, 0))],
        out_specs=pl.BlockSpec((None, N_SMALL, seq), lambda b: (b, 0, 0)),
        scratch_shapes=[pltpu.VMEM((seq, LANES), F32)],
        compiler_params=pltpu.CompilerParams(
            dimension_semantics=("arbitrary",), vmem_limit_bytes=VMEM_LIMIT),
        name="fprep",
    )(small, bias_pad)


def _sublane_tiles(x, op):
    t = x[0:8, :]
    for r in range(1, x.shape[0] // 8):
        t = op(t, x[8 * r:8 * (r + 1), :])
    return t


def _scores(k_refs, qts, h, j, tile, mask):
    s = _dot(k_refs[h][j * tile:(j + 1) * tile, :], qts[h])
    return s if mask is None else jnp.where(mask, s, -jnp.inf)


def _two_pass_sweep(n_tiles, prepare, make_queries, k_refs, vt_refs, scr, tile, mask, finish,
                    early_exp):
    bufs = lambda i: [scr[2 * (i % 2) + h] for h in range(2)]

    def pass1(i):
        prepare(i)
        qts, ms = make_queries(i), [[], []]
        for j in range(i + 1):
            for h in range(2):
                s = _scores(k_refs, qts, h, j, tile, mask if j == i else None)
                t = _sublane_tiles(s, jnp.maximum)
                if early_exp:
                    m = jnp.max(t, axis=0, keepdims=True)
                    m = jnp.maximum(ms[h][-1], m) if ms[h] else m
                    ms[h].append(m)
                    bufs(i)[h][j * tile:(j + 1) * tile, :] = jnp.exp2(s - m).astype(BF16)
                else:
                    ms[h] = [jnp.maximum(ms[h][0], t) if ms[h] else t]
                    bufs(i)[h][j * tile:(j + 1) * tile, :] = s
        if not early_exp:
            ms = [[jnp.max(t[0], axis=0, keepdims=True)] for t in ms]
        return ms

    def pass2(i, ms):
        accs = [None, None]
        for j in range(i + 1):
            for h in range(2):
                w = bufs(i)[h][j * tile:(j + 1) * tile, :]
                if not early_exp:
                    w = jnp.exp2(w - ms[h][0]).astype(BF16)
                pv = _dot(vt_refs[h][:, j * tile:(j + 1) * tile], w)
                if early_exp and j < i:
                    pv = pv * jnp.exp2(ms[h][j] - ms[h][i])
                accs[h] = pv if accs[h] is None else accs[h] + pv
        return accs

    ms = pass1(0)
    for i in range(n_tiles):
        ms_next = pass1(i + 1) if i + 1 < n_tiles else None
        finish(i, pass2(i, ms))
        ms = ms_next


def _fox_kernel(q_ref, k_ref, v_ref, g_ref, ft_ref, o_ref, k_scr, vt_scr, *s_scr, seq, tile):
    lane = lax.broadcasted_iota(jnp.int32, (1, LANES), 1)
    sub = lax.broadcasted_iota(jnp.int32, (LANES, 1), 0)
    key = lax.broadcasted_iota(jnp.int32, (tile, tile), 0)
    qry = lax.broadcasted_iota(jnp.int32, (tile, tile), 1)
    causal = key <= qry
    own = (lane < HEAD_DIM, lane >= HEAD_DIM)
    own_t = (sub < HEAD_DIM, sub >= HEAD_DIM)
    bias_lo = (HEAD_DIM, 0)
    ones_q = tuple(jnp.logical_and(lane >= b, lane < b + 3).astype(F32) for b in bias_lo)

    def prepare(i):
        r0 = i * tile
        k = k_ref[r0:r0 + tile, :]
        vt = v_ref[r0:r0 + tile, :].astype(F32).T
        for hh in range(2):
            b = bias_lo[hh]
            hi, mid, lo = _split3(ft_ref[hh:hh + 1, r0:r0 + tile] * (-LOG2E))
            rows = jnp.concatenate([hi.astype(F32), mid.astype(F32), lo.astype(F32),
                                    jnp.zeros((5, tile), F32)], axis=0)
            block = jnp.concatenate(([jnp.zeros((b, tile), F32)] if b else []) + [rows]
                                    + [jnp.zeros((LANES - b - 8, tile), F32)], axis=0)
            k_scr[hh, r0:r0 + tile, :] = jnp.where(own[hh], k, block.T.astype(BF16))
            vt_scr[hh, :, r0:r0 + tile] = jnp.where(own_t[hh], vt, 1.0).astype(BF16)

    def queries(i):
        q = q_ref[i * tile:(i + 1) * tile, :].astype(F32)
        return [jnp.where(own[hh], q, ones_q[hh]).T.astype(BF16) for hh in range(2)]

    def finish(i, accs):
        r0 = i * tile
        num = jnp.where(own_t[0], accs[0], accs[1])
        den = jnp.where(own_t[0], accs[0][HEAD_DIM:HEAD_DIM + 1, :], accs[1][0:1, :])
        g = g_ref[r0:r0 + tile, :].astype(F32)
        o_ref[r0:r0 + tile, :] = ((num / den).T * _silu(g)).astype(BF16)

    _two_pass_sweep(seq // tile, prepare, queries, [k_scr.at[0], k_scr.at[1]],
                    [vt_scr.at[0], vt_scr.at[1]], s_scr, tile, causal, finish, True)


def _fox(p, ft4, bsz, seq):
    tile = min(ATT_TILE, seq)
    blk = lambda off: pl.BlockSpec((seq, LANES), lambda b, h, off=off: (b, off // LANES + h))
    return pl.pallas_call(
        functools.partial(_fox_kernel, seq=seq, tile=tile),
        out_shape=jax.ShapeDtypeStruct((bsz * seq, D_MODEL), BF16),
        grid=(bsz, D_MODEL // LANES),
        in_specs=[blk(C_FQ), blk(C_FK), blk(C_FV), blk(C_FG),
                  pl.BlockSpec((None, None, 2, seq), lambda b, h: (b, h, 0, 0))],
        out_specs=pl.BlockSpec((seq, LANES), lambda b, h: (b, h)),
        scratch_shapes=[pltpu.VMEM((2, seq, LANES), BF16), pltpu.VMEM((2, LANES, seq), BF16)]
        + [pltpu.VMEM((seq, tile), BF16)] * 4,
        compiler_params=pltpu.CompilerParams(
            dimension_semantics=("arbitrary", "arbitrary"), vmem_limit_bytes=VMEM_LIMIT),
        name="fox",
    )(p, p, p, p, ft4)


def _diff_kernel(q_ref, k_ref, v_ref, g_ref, cos_ref, s1_ref, s2_ref, lam_ref, sub_ref,
                 o_ref, k_scr, vt_scr, *s_scr, seq, tile, lam_init):
    def rope(x, r0):
        return (x * cos_ref[r0:r0 + tile, :]
                + pltpu.roll(x, LANES - ROT_HALF, 1) * s1_ref[r0:r0 + tile, :]
                + pltpu.roll(x, ROT_HALF, 1) * s2_ref[r0:r0 + tile, :])

    def prepare(i):
        r0 = i * tile
        k_scr[r0:r0 + tile, :] = rope(k_ref[r0:r0 + tile, :].astype(F32), r0).astype(BF16)
        vt = v_ref[r0:r0 + tile, :].astype(F32).T
        vt_scr[:, r0:r0 + tile] = jnp.concatenate(
            [vt, jnp.ones((SUM_ROWS, tile), F32)], axis=0).astype(BF16)

    lp = lam_ref[...]
    lam = (jnp.exp(jnp.sum(lp[0:1] * lp[1:2], axis=1, keepdims=True))
           - jnp.exp(jnp.sum(lp[2:3] * lp[3:4], axis=1, keepdims=True)) + lam_init)

    lane = lax.broadcasted_iota(jnp.int32, (1, LANES), 1)
    first = lane < HEAD_DIM
    key = lax.broadcasted_iota(jnp.int32, (tile, tile), 0)
    qry = lax.broadcasted_iota(jnp.int32, (tile, tile), 1)
    chunk_causal = (key // MASK_CHUNK) <= (qry // MASK_CHUNK)

    def queries(i):
        r0 = i * tile
        q = rope(q_ref[r0:r0 + tile, :].astype(F32), r0)
        return [jnp.where(first, q, 0.0).T.astype(BF16), jnp.where(first, 0.0, q).T.astype(BF16)]

    def finish(i, accs):
        r0 = i * tile
        o = (accs[0][0:LANES] / accs[0][LANES:LANES + 1]
             - lam * (accs[1][0:LANES] / accs[1][LANES:LANES + 1])).T
        ms = jnp.mean(o * o, axis=-1, keepdims=True)
        o = o * lax.rsqrt(ms + EPS) * sub_ref[...] * (1.0 - lam_init)
        g = g_ref[r0:r0 + tile, :].astype(F32)
        o_ref[r0:r0 + tile, :] = (o * _silu(g)).astype(BF16)

    _two_pass_sweep(seq // tile, prepare, queries, [k_scr, k_scr], [vt_scr, vt_scr],
                    s_scr, tile, chunk_causal, finish, False)


def _diff(p, cos_t, s1_t, s2_t, lam_p, sub_w, bsz, seq, lam_init):
    tile = min(ATT_TILE, seq)
    blk = lambda off: pl.BlockSpec((seq, LANES), lambda b, h, off=off: (b, off // LANES + h))
    const = lambda shape: pl.BlockSpec(shape, lambda b, h: (0, 0))
    return pl.pallas_call(
        functools.partial(_diff_kernel, seq=seq, tile=tile, lam_init=lam_init),
        out_shape=jax.ShapeDtypeStruct((bsz * seq, D_MODEL), BF16),
        grid=(bsz, DIFF_HEADS),
        in_specs=[blk(C_DQ), blk(C_DK), blk(C_DV), blk(C_DG),
                  const((seq, LANES)), const((seq, LANES)), const((seq, LANES)),
                  const((4, HEAD_DIM)), const((1, LANES))],
        out_specs=pl.BlockSpec((seq, LANES), lambda b, h: (b, h)),
        scratch_shapes=[pltpu.VMEM((seq, LANES), BF16), pltpu.VMEM((LANES + SUM_ROWS, seq), BF16)]
        + [pltpu.VMEM((seq, tile), F32)] * 4,
        compiler_params=pltpu.CompilerParams(
            dimension_semantics=("arbitrary", "arbitrary"), vmem_limit_bytes=VMEM_LIMIT),
        name="diffattn",
    )(p, p, p, p, cos_t, s1_t, s2_t, lam_p, sub_w)


def _ssd_kernel(xbc_ref, z_ref, sm_ref, cw_ref, cb_ref, dtb_ref, alog_ref, dskip_ref, nw_ref,
                e_ref, shift_ref, tril_ref, o_ref, u_scr, st_scr, *, q, nsub):
    @pl.when(pl.program_id(1) == 0)
    def _():
        u_scr[0:CONV_PAD, :] = jnp.zeros((CONV_PAD, 2 * D_MODEL), BF16)
        st_scr[...] = jnp.zeros_like(st_scr)

    rows = q * nsub
    u_scr[CONV_PAD:CONV_PAD + rows, :] = xbc_ref[...]
    lane = lax.broadcasted_iota(jnp.int32, (1, LANES), 1)
    is_head = jnp.logical_and(lane >= N_SMALL, lane < 2 * N_SMALL)
    a_neg = jnp.where(is_head, -jnp.exp(alog_ref[...]), 0.0)
    row = lax.broadcasted_iota(jnp.int32, (q, q), 0)
    col = lax.broadcasted_iota(jnp.int32, (q, q), 1)
    tril = col <= row
    first = lane < HEAD_DIM

    for sc in range(nsub):
        r0 = sc * q
        delayed = _dot(shift_ref[...], u_scr[r0:r0 + CONV_PAD + q, :])
        conv = cb_ref[...] + cw_ref[CONV_K - 1:CONV_K, :] * xbc_ref[r0:r0 + q, :].astype(F32)
        for kk in range(CONV_K - 1):
            conv = conv + cw_ref[kk:kk + 1, :] * delayed[kk * q:(kk + 1) * q, :]
        act = _silu(conv)
        xs = act[:, 0:D_MODEL]

        dt = _softplus(sm_ref[r0:r0 + q, :] + dtb_ref[...])
        a_cs = _exact_dot_l(tril_ref[...], dt * a_neg)
        ea = jnp.exp(a_cs)
        eds = jnp.exp(a_cs[q - 1:q, :] - a_cs)
        stacked = jnp.concatenate([ea, eds * dt], axis=0)
        hi = stacked.astype(BF16)
        mid = (stacked - hi.astype(F32)).astype(BF16)
        wide = _dot(hi, e_ref[...]) + _dot(mid, e_ref[...])
        ea_x, w_x = wide[0:q], wide[q:2 * q]
        a_cs_t = a_cs.T
        dt_t = dt.T

        xs_b = xs.astype(BF16)
        xdt_s = (xs * w_x).astype(BF16)

        ys = []
        for g in range(SSM_GROUPS):
            lo = g * SSM_GROUP_W
            bg = act[:, D_MODEL + g * SSM_STATE:D_MODEL + (g + 1) * SSM_STATE].astype(BF16)
            cg = act[:, D_MODEL + SSM_GROUPS * SSM_STATE + g * SSM_STATE:
                     D_MODEL + SSM_GROUPS * SSM_STATE + (g + 1) * SSM_STATE].astype(BF16)
            cb = _dot_nt(cg, bg)
            state = st_scr[g]
            y_off = _dot(cg, state.astype(BF16)) * ea_x[:, lo:lo + SSM_GROUP_W]
            y_diag = []
            for pp in range(2):
                xp = xs_b[:, lo + pp * LANES:lo + (pp + 1) * LANES]
                halves = []
                for hh in range(2):
                    hl = N_SMALL + g * 4 + pp * 2 + hh
                    seg = a_cs[:, hl:hl + 1] - a_cs_t[hl:hl + 1, :]
                    decay = jnp.where(tril, jnp.exp(seg), 0.0) * dt_t[hl:hl + 1, :]
                    halves.append(_dot((cb * decay).astype(BF16), xp))
                y_diag.append(jnp.where(first, halves[0], halves[1]))
            new_states = lax.dot_general(bg, xdt_s[:, lo:lo + SSM_GROUP_W],
                                         (((0,), (0,)), ((), ())), preferred_element_type=F32)
            st_scr[g] = state * ea_x[q - 1:q, lo:lo + SSM_GROUP_W] + new_states
            ys.append(jnp.concatenate(y_diag, axis=1) + y_off)

        y = jnp.concatenate(ys, axis=1) + xs * dskip_ref[...]
        z = z_ref[r0:r0 + q, :].astype(F32)
        yg = y * _silu(z)
        outs = []
        for g in range(SSM_GROUPS):
            v = yg[:, g * SSM_GROUP_W:(g + 1) * SSM_GROUP_W]
            ms = jnp.mean(v * v, axis=-1, keepdims=True)
            outs.append(v * lax.rsqrt(ms + EPS))
        o_ref[r0:r0 + q, :] = (jnp.concatenate(outs, axis=1) * nw_ref[...]).astype(BF16)

    u_scr[CONV_PAD - CONV_HALO:CONV_PAD, :] = xbc_ref[rows - CONV_HALO:rows, :]


def _conv_shift_matrix(q):
    m = np.zeros((3 * q, CONV_PAD + q), np.float32)
    for k in range(CONV_K - 1):
        t = np.arange(q)
        m[k * q + t, CONV_PAD + t - (CONV_K - 1) + k] = 1.0
    return jnp.asarray(m, BF16)


def _ssd(p, small, conv_w, conv_b, dtb_pad, alog_pad, dskip_x, norm_w, expand, bsz, seq):
    q = min(SSD_CHUNK, seq)
    nsub = SSD_SUBCHUNKS if seq % (q * SSD_SUBCHUNKS) == 0 else 1
    rows = q * nsub
    nc = seq // rows
    const = lambda shape: pl.BlockSpec(shape, lambda b, c: (0, 0))
    return pl.pallas_call(
        functools.partial(_ssd_kernel, q=q, nsub=nsub),
        out_shape=jax.ShapeDtypeStruct((bsz * seq, D_MODEL), BF16),
        grid=(bsz, nc),
        in_specs=[
            pl.BlockSpec((rows, 2 * D_MODEL), lambda b, c: (b * nc + c, C_XBC // (2 * D_MODEL))),
            pl.BlockSpec((rows, D_MODEL), lambda b, c: (b * nc + c, C_SZ // D_MODEL)),
            pl.BlockSpec((rows, LANES), lambda b, c: (b * nc + c, 0)),
            const((CONV_K, 2 * D_MODEL)), const((1, 2 * D_MODEL)),
            const((1, LANES)), const((1, LANES)),
            const((1, D_MODEL)), const((1, D_MODEL)),
            const((LANES, D_MODEL)), const((3 * q, CONV_PAD + q)), const((q, q)),
        ],
        out_specs=pl.BlockSpec((rows, D_MODEL), lambda b, c: (b * nc + c, 0)),
        scratch_shapes=[pltpu.VMEM((CONV_PAD + rows, 2 * D_MODEL), BF16),
                        pltpu.VMEM((SSM_GROUPS, SSM_STATE, SSM_GROUP_W), F32)],
        compiler_params=pltpu.CompilerParams(
            dimension_semantics=("arbitrary", "arbitrary"), vmem_limit_bytes=VMEM_LIMIT),
        name="ssd",
    )(p, p, small, conv_w, conv_b, dtb_pad, alog_pad, dskip_x, norm_w, expand,
      _conv_shift_matrix(q), jnp.asarray(np.tril(np.ones((q, q), np.float32)), BF16))


def _merge_kernel(x_ref, ya_ref, yb_ref, yc_ref, g0_ref, g1_ref, g2_ref, wb_ref, wo_ref,
                  fw_ref, o_ref, *, final):
    merged = None
    for n, (y_ref, g_ref) in enumerate(((ya_ref, g0_ref), (yb_ref, g1_ref), (yc_ref, g2_ref))):
        term = _sigmoid(g_ref[...].astype(F32)) * _dot(y_ref[...], wb_ref[n])
        merged = term if merged is None else merged + term
    out = x_ref[...] + _dot(merged.astype(BF16), wo_ref[...])
    if final:
        ms = jnp.mean(out * out, axis=-1, keepdims=True)
        out = out * lax.rsqrt(ms + EPS) * fw_ref[...]
    o_ref[...] = out


def _merge(x2, y_a, y_b, y_c, p, w_branch, w_out, final_w, final):
    t = x2.shape[0]
    tm = min(MERGE_TM, t)
    rows = lambda: pl.BlockSpec((tm, D_MODEL), lambda i: (i, 0))
    gate = lambda n: pl.BlockSpec((tm, D_MODEL), lambda i, n=n: (i, C_MG // D_MODEL + n))
    return pl.pallas_call(
        functools.partial(_merge_kernel, final=final),
        out_shape=jax.ShapeDtypeStruct((t, D_MODEL), F32),
        grid=(t // tm,),
        in_specs=[rows(), rows(), rows(), rows(), gate(0), gate(1), gate(2),
                  pl.BlockSpec((3, D_MODEL, D_MODEL), lambda i: (0, 0, 0)),
                  pl.BlockSpec((D_MODEL, D_MODEL), lambda i: (0, 0)),
                  pl.BlockSpec((1, D_MODEL), lambda i: (0, 0))],
        out_specs=rows(),
        compiler_params=pltpu.CompilerParams(
            dimension_semantics=("arbitrary",), vmem_limit_bytes=VMEM_LIMIT),
        name="merge",
    )(x2, y_a, y_b, y_c, p, p, p, w_branch, w_out, final_w)


def _rope_tables(seq):
    pos = jnp.arange(seq, dtype=F32)
    inv_freq = ROPE_THETA ** (-jnp.arange(0, 2 * ROT_HALF, 2, dtype=F32) / (2 * ROT_HALF))
    ang = pos[:, None] * inv_freq[None, :]
    cos, sin = jnp.cos(ang), jnp.sin(ang)
    ones = jnp.ones((seq, HEAD_DIM - 2 * ROT_HALF), F32)
    zeros = jnp.zeros((seq, ROT_HALF), F32)
    rest = jnp.zeros((seq, HEAD_DIM - 2 * ROT_HALF), F32)
    cos_c = jnp.concatenate([cos, cos, ones], axis=1)
    s1_c = jnp.concatenate([-sin, zeros, rest], axis=1)
    s2_c = jnp.concatenate([zeros, sin, rest], axis=1)
    two = lambda t: jnp.concatenate([t, t], axis=1)
    return two(cos_c), two(s1_c), two(s2_c)


def _pad_lanes(v, start):
    return jnp.zeros((1, LANES), F32).at[0, start:start + v.shape[0]].set(v.astype(F32))


_SEGMENTS = ((0, C_FQ, 1024, True), (1024, C_FK, 1024, False), (2048, C_FV, 1024, False),
             (3088, C_FG, 1024, False), (5136, C_XBC, 2048, False), (4112, C_SZ, 1024, False),
             (7200, C_DQ, 1024, True), (8224, C_DK, 1024, False), (9248, C_DV, 1024, False),
             (10272, C_DG, 1024, False), (11296, C_MG, 3072, False))
_SRC_FORGET, _SRC_DT, _N_IN = 3072, 7184, 14368


def _wprep_kernel(w_ref, wm_ref, ws_ref):
    scale = LOG2E * HEAD_DIM ** -0.5
    for src, dst, n, scaled in _SEGMENTS:
        x = w_ref[:, src:src + n]
        wm_ref[:, dst:dst + n] = (x * scale if scaled else x).astype(BF16)
    rows = w_ref.shape[0]
    ws_ref[...] = jnp.concatenate(
        [w_ref[:, _SRC_FORGET:_SRC_FORGET + N_SMALL], w_ref[:, _SRC_DT:_SRC_DT + N_SMALL],
         jnp.zeros((rows, LANES - 2 * N_SMALL), F32)], axis=1).astype(BF16)


def _projection_weights(w_in):
    depth = w_in.shape[0]
    tk = 128
    return pl.pallas_call(
        _wprep_kernel,
        out_shape=(jax.ShapeDtypeStruct((depth, D_MODEL, N_MAIN), BF16),
                   jax.ShapeDtypeStruct((depth, D_MODEL, LANES), BF16)),
        grid=(depth, D_MODEL // tk),
        in_specs=[pl.BlockSpec((None, tk, _N_IN), lambda l, i: (l, i, 0))],
        out_specs=(pl.BlockSpec((None, tk, N_MAIN), lambda l, i: (l, i, 0)),
                   pl.BlockSpec((None, tk, LANES), lambda l, i: (l, i, 0))),
        compiler_params=pltpu.CompilerParams(
            dimension_semantics=("arbitrary", "arbitrary"), vmem_limit_bytes=VMEM_LIMIT),
        name="wprep",
    )(w_in)


def kernel(x, norm_w, w_in, b_forget, conv_w, conv_b, dt_bias, a_log, d_skip, ssm_norm_w,
           diff_lambda, subln_w, w_branch, w_out, final_norm_w):
    bsz, seq, _ = x.shape
    depth = norm_w.shape[0]
    cos_t, s1_t, s2_t = _rope_tables(seq)
    heads = jnp.arange(D_MODEL) // HEAD_DIM
    expand = (jnp.arange(LANES)[:, None] == (heads[None, :] + N_SMALL)).astype(BF16)

    w_main, w_small = _projection_weights(w_in)
    w_branch_b, w_out_b = w_branch.astype(BF16), w_out.astype(BF16)

    x2 = x.reshape(bsz * seq, D_MODEL)
    for layer in range(depth):
        p, small = _inproj(x2, norm_w[layer][None, :], w_main[layer], w_small[layer])

        ft = _fprep(small, _pad_lanes(b_forget[layer], 0), bsz, seq)
        y_a = _fox(p, ft.reshape(bsz, N_SMALL // 2, 2, seq), bsz, seq)

        y_b = _ssd(p, small, conv_w[layer], conv_b[layer][None, :],
                   _pad_lanes(dt_bias[layer], N_SMALL),
                   _pad_lanes(a_log[layer], N_SMALL),
                   jnp.repeat(d_skip[layer], HEAD_DIM)[None, :],
                   ssm_norm_w[layer][None, :], expand, bsz, seq)

        lam_init = 0.8 - 0.6 * math.exp(-0.3 * layer)
        y_c = _diff(p, cos_t, s1_t, s2_t, diff_lambda[layer], subln_w[layer][None, :],
                    bsz, seq, lam_init)

        x2 = _merge(x2, y_a, y_b, y_c, p, w_branch_b[layer], w_out_b[layer],
                    final_norm_w[None, :], layer == depth - 1)
    return x2.reshape(bsz, seq, D_MODEL)
```

```python
import functools
import math

import numpy as np
import jax
import jax.numpy as jnp
from jax import lax
from jax.experimental import pallas as pl
from jax.experimental.pallas import tpu as pltpu

F32 = jnp.float32
BF16 = jnp.bfloat16

D_MODEL = 1024
EPS = 1e-6
LOG2E = 1.4426950408889634
HEAD_DIM = 64
LANES = 128
N_SMALL = 16
SSM_GROUPS = 4
SSM_STATE = 128
SSM_GROUP_W = 256
CONV_K = 4
CONV_HALO = 16
CONV_PAD = 128
DIFF_HEADS = 8
ROT_HALF = 8
ROPE_THETA = 500000.0
MASK_CHUNK = 64
SUM_ROWS = 16

C_FQ, C_FK, C_FV, C_FG = 0, 1024, 2048, 3072
C_XBC, C_SZ = 4096, 6144
C_DQ, C_DK, C_DV, C_DG = 7168, 8192, 9216, 10240
C_MG = 11264
N_MAIN = 14336

ATT_TILE = 256
FOX_PAIRS = 2
DIFF_GROUP = 2
SSD_CHUNK = 128
SSD_SUBCHUNKS = 4
PROJ_TM, PROJ_TN = 2048, 1024
MERGE_TM = 512
VMEM_LIMIT = 56 * 1024 * 1024


def _sigmoid(x):
    return 0.5 * jnp.tanh(0.5 * x) + 0.5


def _silu(x):
    h = 0.5 * x
    return h + h * jnp.tanh(h)


def _softplus(x):
    return jnp.maximum(x, 0.0) + jnp.log1p(jnp.exp(-jnp.abs(x)))


def _split3(x):
    hi = x.astype(BF16)
    r1 = x - hi.astype(F32)
    mid = r1.astype(BF16)
    lo = (r1 - mid.astype(F32)).astype(BF16)
    return hi, mid, lo


def _dot(a, b):
    return jnp.dot(a, b, preferred_element_type=F32)


def _dot_nt(a, b):
    return lax.dot_general(a, b, (((1,), (1,)), ((), ())), preferred_element_type=F32)


def _exact_dot_l(m01, x):
    hi, mid, lo = _split3(x)
    return _dot(m01, hi) + _dot(m01, mid) + _dot(m01, lo)


def _tril_bf16(n):
    r = lax.broadcasted_iota(jnp.int32, (n, n), 0)
    c = lax.broadcasted_iota(jnp.int32, (n, n), 1)
    return jnp.where(r >= c, 1.0, 0.0).astype(BF16)


def _inproj_kernel(x_ref, nw_ref, w_ref, ws_ref, p_ref, s_ref, h_ref):
    @pl.when(pl.program_id(1) == 0)
    def _():
        x = x_ref[...]
        ms = jnp.mean(x * x, axis=-1, keepdims=True)
        h = (x * lax.rsqrt(ms + EPS) * nw_ref[...]).astype(BF16)
        h_ref[...] = h
        s_ref[...] = _dot_nt(h, ws_ref[...])

    p_ref[...] = _dot_nt(h_ref[...], w_ref[...]).astype(BF16)


def _inproj(x2, norm_w, w_main_t, w_small_t):
    t = x2.shape[0]
    tm, tn = min(PROJ_TM, t), PROJ_TN
    return pl.pallas_call(
        _inproj_kernel,
        out_shape=(jax.ShapeDtypeStruct((t, N_MAIN), BF16),
                   jax.ShapeDtypeStruct((t, LANES), F32)),
        grid=(t // tm, N_MAIN // tn),
        in_specs=[
            pl.BlockSpec((tm, D_MODEL), lambda i, j: (i, 0)),
            pl.BlockSpec((1, D_MODEL), lambda i, j: (0, 0)),
            pl.BlockSpec((tn, D_MODEL), lambda i, j: (j, 0)),
            pl.BlockSpec((LANES, D_MODEL), lambda i, j: (0, 0)),
        ],
        out_specs=(pl.BlockSpec((tm, tn), lambda i, j: (i, j)),
                   pl.BlockSpec((tm, LANES), lambda i, j: (i, 0))),
        scratch_shapes=[pltpu.VMEM((tm, D_MODEL), BF16)],
        compiler_params=pltpu.CompilerParams(
            dimension_semantics=("arbitrary", "arbitrary"), vmem_limit_bytes=VMEM_LIMIT),
        name="inproj",
    )(x2, norm_w, w_main_t, w_small_t)


def _fprep_kernel(s_ref, bias_ref, ft_ref, f_scr, *, seq):
    tril = _tril_bf16(LANES)
    carry = jnp.zeros((1, LANES), F32)
    for blk in range(seq // LANES):
        x = s_ref[blk * LANES:(blk + 1) * LANES, :] + bias_ref[...]
        log_f = jnp.minimum(x, 0.0) - jnp.log1p(jnp.exp(-jnp.abs(x)))
        c = _exact_dot_l(tril, log_f) + carry
        f_scr[blk * LANES:(blk + 1) * LANES, :] = c
        carry = c[LANES - 1:LANES, :]
    ft_ref[...] = f_scr[...].T[0:N_SMALL, :]


def _fprep(small, bias_pad, bsz, seq):
    return pl.pallas_call(
        functools.partial(_fprep_kernel, seq=seq),
        out_shape=jax.ShapeDtypeStruct((bsz, N_SMALL, seq), F32),
        grid=(bsz,),
        in_specs=[pl.BlockSpec((seq, LANES), lambda b: (b, 0)),
                  pl.BlockSpec((1, LANES), lambda b: (0, 0))],
        out_specs=pl.BlockSpec((None, N_SMALL, seq), lambda b: (b, 0, 0)),
        scratch_shapes=[pltpu.VMEM((seq, LANES), F32)],
        compiler_params=pltpu.CompilerParams(
            dimension_semantics=("arbitrary",), vmem_limit_bytes=VMEM_LIMIT),
        name="fprep",
    )(small, bias_pad)


def _sublane_tiles(x, op):
    t = x[0:8, :]
    for r in range(1, x.shape[0] // 8):
        t = op(t, x[8 * r:8 * (r + 1), :])
    return t


def _scores(k_refs, qts, h, j, tile, mask):
    s = _dot(k_refs[h][j * tile:(j + 1) * tile, :], qts[h])
    return s if mask is None else jnp.where(mask, s, -jnp.inf)


def _two_pass_sweep(n_tiles, prepare, make_queries, k_refs, vt_refs, scr, tile, mask, finish,
                    early_exp):
    n = len(k_refs)
    bufs = lambda i: [scr[n * (i % 2) + h] for h in range(n)]

    def pass1(i):
        prepare(i)
        qts, ms = make_queries(i), [[] for _ in range(n)]
        for j in range(i + 1):
            for h in range(n):
                s = _scores(k_refs, qts, h, j, tile, mask if j == i else None)
                t = _sublane_tiles(s, jnp.maximum)
                if early_exp:
                    m = jnp.max(t, axis=0, keepdims=True)
                    m = jnp.maximum(ms[h][-1], m) if ms[h] else m
                    ms[h].append(m)
                    bufs(i)[h][j * tile:(j + 1) * tile, :] = jnp.exp2(s - m).astype(BF16)
                else:
                    ms[h] = [jnp.maximum(ms[h][0], t) if ms[h] else t]
                    bufs(i)[h][j * tile:(j + 1) * tile, :] = s
        if not early_exp:
            ms = [[jnp.max(t[0], axis=0, keepdims=True)] for t in ms]
        return ms

    def pass2(i, ms):
        accs = [None] * n
        for j in range(i + 1):
            for h in range(n):
                w = bufs(i)[h][j * tile:(j + 1) * tile, :]
                if not early_exp:
                    w = jnp.exp2(w - ms[h][0]).astype(BF16)
                pv = _dot(vt_refs[h][:, j * tile:(j + 1) * tile], w)
                if early_exp and j < i:
                    pv = pv * jnp.exp2(ms[h][j] - ms[h][i])
                accs[h] = pv if accs[h] is None else accs[h] + pv
        return accs

    ms = pass1(0)
    for i in range(n_tiles):
        ms_next = pass1(i + 1) if i + 1 < n_tiles else None
        finish(i, pass2(i, ms))
        ms = ms_next


def _fox_kernel(q_ref, k_ref, v_ref, g_ref, ft_ref, o_ref, k_scr, vt_scr, *s_scr, seq, tile):
    lane = lax.broadcasted_iota(jnp.int32, (1, LANES), 1)
    sub = lax.broadcasted_iota(jnp.int32, (LANES, 1), 0)
    key = lax.broadcasted_iota(jnp.int32, (tile, tile), 0)
    qry = lax.broadcasted_iota(jnp.int32, (tile, tile), 1)
    causal = key <= qry
    own = (lane < HEAD_DIM, lane >= HEAD_DIM)
    own_t = (sub < HEAD_DIM, sub >= HEAD_DIM)
    bias_lo = (HEAD_DIM, 0)
    ones_q = tuple(jnp.logical_and(lane >= b, lane < b + 3).astype(F32) for b in bias_lo)
    streams = [(pr, hh) for pr in range(FOX_PAIRS) for hh in range(2)]

    def prepare(i):
        r0 = i * tile
        for pr in range(FOX_PAIRS):
            k = k_ref[r0:r0 + tile, pr * LANES:(pr + 1) * LANES]
            vt = v_ref[r0:r0 + tile, pr * LANES:(pr + 1) * LANES].astype(F32).T
            for hh in range(2):
                b = bias_lo[hh]
                hi, mid, lo = _split3(ft_ref[pr, hh:hh + 1, r0:r0 + tile] * (-LOG2E))
                rows = jnp.concatenate([hi.astype(F32), mid.astype(F32), lo.astype(F32),
                                        jnp.zeros((5, tile), F32)], axis=0)
                block = jnp.concatenate(([jnp.zeros((b, tile), F32)] if b else []) + [rows]
                                        + [jnp.zeros((LANES - b - 8, tile), F32)], axis=0)
                st = 2 * pr + hh
                k_scr[st, r0:r0 + tile, :] = jnp.where(own[hh], k, block.T.astype(BF16))
                vt_scr[st, :, r0:r0 + tile] = jnp.where(own_t[hh], vt, 1.0).astype(BF16)

    def queries(i):
        out = []
        for pr, hh in streams:
            q = q_ref[i * tile:(i + 1) * tile, pr * LANES:(pr + 1) * LANES].astype(F32)
            out.append(jnp.where(own[hh], q, ones_q[hh]).T.astype(BF16))
        return out

    def finish(i, accs):
        r0 = i * tile
        for pr in range(FOX_PAIRS):
            a0, a1 = accs[2 * pr], accs[2 * pr + 1]
            num = jnp.where(own_t[0], a0, a1)
            den = jnp.where(own_t[0], a0[HEAD_DIM:HEAD_DIM + 1, :], a1[0:1, :])
            g = g_ref[r0:r0 + tile, pr * LANES:(pr + 1) * LANES].astype(F32)
            o_ref[r0:r0 + tile, pr * LANES:(pr + 1) * LANES] = (
                (num / den).T * _silu(g)).astype(BF16)

    n = len(streams)
    _two_pass_sweep(seq // tile, prepare, queries, [k_scr.at[st] for st in range(n)],
                    [vt_scr.at[st] for st in range(n)], s_scr, tile, causal, finish, True)


def _fox(p, ft4, bsz, seq):
    tile = min(ATT_TILE, seq)
    w = FOX_PAIRS * LANES
    blk = lambda off: pl.BlockSpec((seq, w), lambda b, h, off=off: (b, off // w + h))
    n = 2 * FOX_PAIRS
    return pl.pallas_call(
        functools.partial(_fox_kernel, seq=seq, tile=tile),
        out_shape=jax.ShapeDtypeStruct((bsz * seq, D_MODEL), BF16),
        grid=(bsz, D_MODEL // w),
        in_specs=[blk(C_FQ), blk(C_FK), blk(C_FV), blk(C_FG),
                  pl.BlockSpec((None, FOX_PAIRS, 2, seq), lambda b, h: (b, h, 0, 0))],
        out_specs=pl.BlockSpec((seq, w), lambda b, h: (b, h)),
        scratch_shapes=[pltpu.VMEM((n, seq, LANES), BF16), pltpu.VMEM((n, LANES, seq), BF16)]
        + [pltpu.VMEM((seq, tile), BF16)] * (2 * n),
        compiler_params=pltpu.CompilerParams(
            dimension_semantics=("arbitrary", "arbitrary"), vmem_limit_bytes=VMEM_LIMIT),
        name="fox",
    )(p, p, p, p, ft4)


def _diff_kernel(q_ref, k_ref, v_ref, g_ref, cos_ref, s1_ref, s2_ref, lam_ref, sub_ref,
                 o_ref, k_scr, vt_scr, *s_scr, seq, tile, lam_init):
    def rope(x, r0):
        return (x * cos_ref[r0:r0 + tile, :]
                + pltpu.roll(x, LANES - ROT_HALF, 1) * s1_ref[r0:r0 + tile, :]
                + pltpu.roll(x, ROT_HALF, 1) * s2_ref[r0:r0 + tile, :])

    def prepare(i):
        r0 = i * tile
        for hd in range(DIFF_GROUP):
            cols = slice(hd * LANES, (hd + 1) * LANES)
            k_scr[hd, r0:r0 + tile, :] = rope(k_ref[r0:r0 + tile, cols].astype(F32), r0).astype(BF16)
            vt = v_ref[r0:r0 + tile, cols].astype(F32).T
            vt_scr[hd, :, r0:r0 + tile] = jnp.concatenate(
                [vt, jnp.ones((SUM_ROWS, tile), F32)], axis=0).astype(BF16)

    lp = lam_ref[...]
    lam = (jnp.exp(jnp.sum(lp[0:1] * lp[1:2], axis=1, keepdims=True))
           - jnp.exp(jnp.sum(lp[2:3] * lp[3:4], axis=1, keepdims=True)) + lam_init)

    lane = lax.broadcasted_iota(jnp.int32, (1, LANES), 1)
    first = lane < HEAD_DIM
    key = lax.broadcasted_iota(jnp.int32, (tile, tile), 0)
    qry = lax.broadcasted_iota(jnp.int32, (tile, tile), 1)
    chunk_causal = (key // MASK_CHUNK) <= (qry // MASK_CHUNK)

    def queries(i):
        r0 = i * tile
        out = []
        for hd in range(DIFF_GROUP):
            q = rope(q_ref[r0:r0 + tile, hd * LANES:(hd + 1) * LANES].astype(F32), r0)
            out += [jnp.where(first, q, 0.0).T.astype(BF16), jnp.where(first, 0.0, q).T.astype(BF16)]
        return out

    def finish(i, accs):
        r0 = i * tile
        for hd in range(DIFF_GROUP):
            a0, a1 = accs[2 * hd], accs[2 * hd + 1]
            o = (a0[0:LANES] / a0[LANES:LANES + 1] - lam * (a1[0:LANES] / a1[LANES:LANES + 1])).T
            ms = jnp.mean(o * o, axis=-1, keepdims=True)
            o = o * lax.rsqrt(ms + EPS) * sub_ref[...] * (1.0 - lam_init)
            g = g_ref[r0:r0 + tile, hd * LANES:(hd + 1) * LANES].astype(F32)
            o_ref[r0:r0 + tile, hd * LANES:(hd + 1) * LANES] = (o * _silu(g)).astype(BF16)

    heads = [hd for hd in range(DIFF_GROUP) for _ in range(2)]
    _two_pass_sweep(seq // tile, prepare, queries, [k_scr.at[hd] for hd in heads],
                    [vt_scr.at[hd] for hd in heads], s_scr, tile, chunk_causal, finish, False)


def _diff(p, cos_t, s1_t, s2_t, lam_p, sub_w, bsz, seq, lam_init):
    tile = min(ATT_TILE, seq)
    w = DIFF_GROUP * LANES
    blk = lambda off: pl.BlockSpec((seq, w), lambda b, h, off=off: (b, off // w + h))
    const = lambda shape: pl.BlockSpec(shape, lambda b, h: (0, 0))
    return pl.pallas_call(
        functools.partial(_diff_kernel, seq=seq, tile=tile, lam_init=lam_init),
        out_shape=jax.ShapeDtypeStruct((bsz * seq, D_MODEL), BF16),
        grid=(bsz, D_MODEL // w),
        in_specs=[blk(C_DQ), blk(C_DK), blk(C_DV), blk(C_DG),
                  const((seq, LANES)), const((seq, LANES)), const((seq, LANES)),
                  const((4, HEAD_DIM)), const((1, LANES))],
        out_specs=pl.BlockSpec((seq, w), lambda b, h: (b, h)),
        scratch_shapes=[pltpu.VMEM((DIFF_GROUP, seq, LANES), BF16),
                        pltpu.VMEM((DIFF_GROUP, LANES + SUM_ROWS, seq), BF16)]
        + [pltpu.VMEM((seq, tile), F32)] * (4 * DIFF_GROUP),
        compiler_params=pltpu.CompilerParams(
            dimension_semantics=("arbitrary", "arbitrary"), vmem_limit_bytes=VMEM_LIMIT),
        name="diffattn",
    )(p, p, p, p, cos_t, s1_t, s2_t, lam_p, sub_w)


def _ssd_kernel(xbc_ref, z_ref, sm_ref, cw_ref, cb_ref, dtb_ref, alog_ref, dskip_ref, nw_ref,
                e_ref, shift_ref, tril_ref, o_ref, u_scr, st_scr, *, q, nsub):
    @pl.when(pl.program_id(1) == 0)
    def _():
        u_scr[0:CONV_PAD, :] = jnp.zeros((CONV_PAD, 2 * D_MODEL), BF16)
        st_scr[...] = jnp.zeros_like(st_scr)

    rows = q * nsub
    u_scr[CONV_PAD:CONV_PAD + rows, :] = xbc_ref[...]
    lane = lax.broadcasted_iota(jnp.int32, (1, LANES), 1)
    is_head = jnp.logical_and(lane >= N_SMALL, lane < 2 * N_SMALL)
    a_neg = jnp.where(is_head, -jnp.exp(alog_ref[...]), 0.0)
    row = lax.broadcasted_iota(jnp.int32, (q, q), 0)
    col = lax.broadcasted_iota(jnp.int32, (q, q), 1)
    tril = col <= row
    first = lane < HEAD_DIM

    for sc in range(nsub):
        r0 = sc * q
        delayed = _dot(shift_ref[...], u_scr[r0:r0 + CONV_PAD + q, :])
        conv = cb_ref[...] + cw_ref[CONV_K - 1:CONV_K, :] * xbc_ref[r0:r0 + q, :].astype(F32)
        for kk in range(CONV_K - 1):
            conv = conv + cw_ref[kk:kk + 1, :] * delayed[kk * q:(kk + 1) * q, :]
        act = _silu(conv)
        xs = act[:, 0:D_MODEL]

        dt = _softplus(sm_ref[r0:r0 + q, :] + dtb_ref[...])
        a_cs = _exact_dot_l(tril_ref[...], dt * a_neg)
        ea = jnp.exp(a_cs)
        eds = jnp.exp(a_cs[q - 1:q, :] - a_cs)
        stacked = jnp.concatenate([ea, eds * dt], axis=0)
        hi = stacked.astype(BF16)
        mid = (stacked - hi.astype(F32)).astype(BF16)
        wide = _dot(hi, e_ref[...]) + _dot(mid, e_ref[...])
        ea_x, w_x = wide[0:q], wide[q:2 * q]
        a_cs_t = a_cs.T
        dt_t = dt.T

        xs_b = xs.astype(BF16)
        xdt_s = (xs * w_x).astype(BF16)

        ys = []
        for g in range(SSM_GROUPS):
            lo = g * SSM_GROUP_W
            bg = act[:, D_MODEL + g * SSM_STATE:D_MODEL + (g + 1) * SSM_STATE].astype(BF16)
            cg = act[:, D_MODEL + SSM_GROUPS * SSM_STATE + g * SSM_STATE:
                     D_MODEL + SSM_GROUPS * SSM_STATE + (g + 1) * SSM_STATE].astype(BF16)
            cb = _dot_nt(cg, bg)
            state = st_scr[g]
            y_off = _dot(cg, state.astype(BF16)) * ea_x[:, lo:lo + SSM_GROUP_W]
            y_diag = []
            for pp in range(2):
                xp = xs_b[:, lo + pp * LANES:lo + (pp + 1) * LANES]
                halves = []
                for hh in range(2):
                    hl = N_SMALL + g * 4 + pp * 2 + hh
                    seg = a_cs[:, hl:hl + 1] - a_cs_t[hl:hl + 1, :]
                    decay = jnp.where(tril, jnp.exp(seg), 0.0) * dt_t[hl:hl + 1, :]
                    halves.append(_dot((cb * decay).astype(BF16), xp))
                y_diag.append(jnp.where(first, halves[0], halves[1]))
            new_states = lax.dot_general(bg, xdt_s[:, lo:lo + SSM_GROUP_W],
                                         (((0,), (0,)), ((), ())), preferred_element_type=F32)
            st_scr[g] = state * ea_x[q - 1:q, lo:lo + SSM_GROUP_W] + new_states
            ys.append(jnp.concatenate(y_diag, axis=1) + y_off)

        y = jnp.concatenate(ys, axis=1) + xs * dskip_ref[...]
        z = z_ref[r0:r0 + q, :].astype(F32)
        yg = y * _silu(z)
        outs = []
        for g in range(SSM_GROUPS):
            v = yg[:, g * SSM_GROUP_W:(g + 1) * SSM_GROUP_W]
            ms = jnp.mean(v * v, axis=-1, keepdims=True)
            outs.append(v * lax.rsqrt(ms + EPS))
        o_ref[r0:r0 + q, :] = (jnp.concatenate(outs, axis=1) * nw_ref[...]).astype(BF16)

    u_scr[CONV_PAD - CONV_HALO:CONV_PAD, :] = xbc_ref[rows - CONV_HALO:rows, :]


def _conv_shift_matrix(q):
    m = np.zeros((3 * q, CONV_PAD + q), np.float32)
    for k in range(CONV_K - 1):
        t = np.arange(q)
        m[k * q + t, CONV_PAD + t - (CONV_K - 1) + k] = 1.0
    return jnp.asarray(m, BF16)


def _ssd(p, small, conv_w, conv_b, dtb_pad, alog_pad, dskip_x, norm_w, expand, bsz, seq):
    q = min(SSD_CHUNK, seq)
    nsub = SSD_SUBCHUNKS if seq % (q * SSD_SUBCHUNKS) == 0 else 1
    rows = q * nsub
    nc = seq // rows
    const = lambda shape: pl.BlockSpec(shape, lambda b, c: (0, 0))
    return pl.pallas_call(
        functools.partial(_ssd_kernel, q=q, nsub=nsub),
        out_shape=jax.ShapeDtypeStruct((bsz * seq, D_MODEL), BF16),
        grid=(bsz, nc),
        in_specs=[
            pl.BlockSpec((rows, 2 * D_MODEL), lambda b, c: (b * nc + c, C_XBC // (2 * D_MODEL))),
            pl.BlockSpec((rows, D_MODEL), lambda b, c: (b * nc + c, C_SZ // D_MODEL)),
            pl.BlockSpec((rows, LANES), lambda b, c: (b * nc + c, 0)),
            const((CONV_K, 2 * D_MODEL)), const((1, 2 * D_MODEL)),
            const((1, LANES)), const((1, LANES)),
            const((1, D_MODEL)), const((1, D_MODEL)),
            const((LANES, D_MODEL)), const((3 * q, CONV_PAD + q)), const((q, q)),
        ],
        out_specs=pl.BlockSpec((rows, D_MODEL), lambda b, c: (b * nc + c, 0)),
        scratch_shapes=[pltpu.VMEM((CONV_PAD + rows, 2 * D_MODEL), BF16),
                        pltpu.VMEM((SSM_GROUPS, SSM_STATE, SSM_GROUP_W), F32)],
        compiler_params=pltpu.CompilerParams(
            dimension_semantics=("arbitrary", "arbitrary"), vmem_limit_bytes=VMEM_LIMIT),
        name="ssd",
    )(p, p, small, conv_w, conv_b, dtb_pad, alog_pad, dskip_x, norm_w, expand,
      _conv_shift_matrix(q), jnp.asarray(np.tril(np.ones((q, q), np.float32)), BF16))


def _merge_kernel(x_ref, ya_ref, yb_ref, yc_ref, g0_ref, g1_ref, g2_ref, wb_ref, wo_ref,
                  fw_ref, o_ref, *, final):
    merged = None
    for n, (y_ref, g_ref) in enumerate(((ya_ref, g0_ref), (yb_ref, g1_ref), (yc_ref, g2_ref))):
        term = _sigmoid(g_ref[...].astype(F32)) * _dot(y_ref[...], wb_ref[n])
        merged = term if merged is None else merged + term
    out = x_ref[...] + _dot(merged.astype(BF16), wo_ref[...])
    if final:
        ms = jnp.mean(out * out, axis=-1, keepdims=True)
        out = out * lax.rsqrt(ms + EPS) * fw_ref[...]
    o_ref[...] = out


def _merge(x2, y_a, y_b, y_c, p, w_branch, w_out, final_w, final):
    t = x2.shape[0]
    tm = min(MERGE_TM, t)
    rows = lambda: pl.BlockSpec((tm, D_MODEL), lambda i: (i, 0))
    gate = lambda n: pl.BlockSpec((tm, D_MODEL), lambda i, n=n: (i, C_MG // D_MODEL + n))
    return pl.pallas_call(
        functools.partial(_merge_kernel, final=final),
        out_shape=jax.ShapeDtypeStruct((t, D_MODEL), F32),
        grid=(t // tm,),
        in_specs=[rows(), rows(), rows(), rows(), gate(0), gate(1), gate(2),
                  pl.BlockSpec((3, D_MODEL, D_MODEL), lambda i: (0, 0, 0)),
                  pl.BlockSpec((D_MODEL, D_MODEL), lambda i: (0, 0)),
                  pl.BlockSpec((1, D_MODEL), lambda i: (0, 0))],
        out_specs=rows(),
        compiler_params=pltpu.CompilerParams(
            dimension_semantics=("arbitrary",), vmem_limit_bytes=VMEM_LIMIT),
        name="merge",
    )(x2, y_a, y_b, y_c, p, p, p, w_branch, w_out, final_w)


def _rope_tables(seq):
    pos = jnp.arange(seq, dtype=F32)
    inv_freq = ROPE_THETA ** (-jnp.arange(0, 2 * ROT_HALF, 2, dtype=F32) / (2 * ROT_HALF))
    ang = pos[:, None] * inv_freq[None, :]
    cos, sin = jnp.cos(ang), jnp.sin(ang)
    ones = jnp.ones((seq, HEAD_DIM - 2 * ROT_HALF), F32)
    zeros = jnp.zeros((seq, ROT_HALF), F32)
    rest = jnp.zeros((seq, HEAD_DIM - 2 * ROT_HALF), F32)
    cos_c = jnp.concatenate([cos, cos, ones], axis=1)
    s1_c = jnp.concatenate([-sin, zeros, rest], axis=1)
    s2_c = jnp.concatenate([zeros, sin, rest], axis=1)
    two = lambda t: jnp.concatenate([t, t], axis=1)
    return two(cos_c), two(s1_c), two(s2_c)


def _pad_lanes(v, start):
    return jnp.zeros((1, LANES), F32).at[0, start:start + v.shape[0]].set(v.astype(F32))


_SEGMENTS = ((0, C_FQ, 1024, True), (1024, C_FK, 1024, False), (2048, C_FV, 1024, False),
             (3088, C_FG, 1024, False), (5136, C_XBC, 2048, False), (4112, C_SZ, 1024, False),
             (7200, C_DQ, 1024, True), (8224, C_DK, 1024, False), (9248, C_DV, 1024, False),
             (10272, C_DG, 1024, False), (11296, C_MG, 3072, False))
_SRC_FORGET, _SRC_DT, _N_IN = 3072, 7184, 14368


def _wprep_kernel(wt_ref, wm_ref, ws_ref):
    scale = LOG2E * HEAD_DIM ** -0.5
    for src, dst, n, scaled in _SEGMENTS:
        x = wt_ref[src:src + n, :]
        wm_ref[dst:dst + n, :] = (x * scale if scaled else x).astype(BF16)
    ws_ref[...] = jnp.concatenate(
        [wt_ref[_SRC_FORGET:_SRC_FORGET + N_SMALL, :], wt_ref[_SRC_DT:_SRC_DT + N_SMALL, :],
         jnp.zeros((LANES - 2 * N_SMALL, wt_ref.shape[1]), F32)], axis=0).astype(BF16)


def _projection_weights(w_in_t, layer):
    return pl.pallas_call(
        _wprep_kernel,
        out_shape=(jax.ShapeDtypeStruct((N_MAIN, D_MODEL), BF16),
                   jax.ShapeDtypeStruct((LANES, D_MODEL), BF16)),
        grid=(D_MODEL // LANES,),
        in_specs=[pl.BlockSpec((None, _N_IN, LANES), lambda i: (layer, 0, i))],
        out_specs=(pl.BlockSpec((N_MAIN, LANES), lambda i: (0, i)),
                   pl.BlockSpec((LANES, LANES), lambda i: (0, i))),
        compiler_params=pltpu.CompilerParams(
            dimension_semantics=("arbitrary",), vmem_limit_bytes=VMEM_LIMIT),
        name="wprep",
    )(w_in_t)


def kernel(x, norm_w, w_in, b_forget, conv_w, conv_b, dt_bias, a_log, d_skip, ssm_norm_w,
           diff_lambda, subln_w, w_branch, w_out, final_norm_w):
    bsz, seq, _ = x.shape
    depth = norm_w.shape[0]
    cos_t, s1_t, s2_t = _rope_tables(seq)
    heads = jnp.arange(D_MODEL) // HEAD_DIM
    expand = (jnp.arange(LANES)[:, None] == (heads[None, :] + N_SMALL)).astype(BF16)

    w_in_t = jnp.swapaxes(w_in, 1, 2)
    w_branch_b, w_out_b = w_branch.astype(BF16), w_out.astype(BF16)

    x2 = x.reshape(bsz * seq, D_MODEL)
    for layer in range(depth):
        w_main_t, w_small_t = _projection_weights(w_in_t, layer)
        p, small = _inproj(x2, norm_w[layer][None, :], w_main_t, w_small_t)

        ft = _fprep(small, _pad_lanes(b_forget[layer], 0), bsz, seq)
        y_a = _fox(p, ft.reshape(bsz, N_SMALL // 2, 2, seq), bsz, seq)

        y_b = _ssd(p, small, conv_w[layer], conv_b[layer][None, :],
                   _pad_lanes(dt_bias[layer], N_SMALL),
                   _pad_lanes(a_log[layer], N_SMALL),
                   jnp.repeat(d_skip[layer], HEAD_DIM)[None, :],
                   ssm_norm_w[layer][None, :], expand, bsz, seq)

        lam_init = 0.8 - 0.6 * math.exp(-0.3 * layer)
        y_c = _diff(p, cos_t, s1_t, s2_t, diff_lambda[layer], subln_w[layer][None, :],
                    bsz, seq, lam_init)

        x2 = _merge(x2, y_a, y_b, y_c, p, w_branch_b[layer], w_out_b[layer],
                    final_norm_w[None, :], layer == depth - 1)
    return x2.reshape(bsz, seq, D_MODEL)
```

```python
import functools
import math

import numpy as np
import jax
import jax.numpy as jnp
from jax import lax
from jax.experimental import pallas as pl
from jax.experimental.pallas import tpu as pltpu

F32 = jnp.float32
BF16 = jnp.bfloat16

D_MODEL = 1024
EPS = 1e-6
LOG2E = 1.4426950408889634
HEAD_DIM = 64
LANES = 128
N_SMALL = 16
SSM_GROUPS = 4
SSM_STATE = 128
SSM_GROUP_W = 256
CONV_K = 4
CONV_HALO = 16
CONV_PAD = 128
DIFF_HEADS = 8
ROT_HALF = 8
ROPE_THETA = 500000.0
MASK_CHUNK = 64
SUM_ROWS = 16

C_FQ, C_FK, C_FV, C_FG = 0, 1024, 2048, 3072
C_XBC, C_SZ = 4096, 6144
C_DQ, C_DK, C_DV, C_DG = 7168, 8192, 9216, 10240
C_MG = 11264
N_MAIN = 14336

ATT_TILE = 256
FOX_PAIRS = 2
DIFF_GROUP = 2
SSD_CHUNK = 128
SSD_SUBCHUNKS = 4
PROJ_TM, PROJ_TN = 2048, 1024
MERGE_TM = 512
VMEM_LIMIT = 56 * 1024 * 1024


def _sigmoid(x):
    return 0.5 * jnp.tanh(0.5 * x) + 0.5


def _silu(x):
    h = 0.5 * x
    return h + h * jnp.tanh(h)


def _softplus(x):
    return jnp.maximum(x, 0.0) + jnp.log1p(jnp.exp(-jnp.abs(x)))


def _split3(x):
    hi = x.astype(BF16)
    r1 = x - hi.astype(F32)
    mid = r1.astype(BF16)
    lo = (r1 - mid.astype(F32)).astype(BF16)
    return hi, mid, lo


def _dot(a, b):
    return jnp.dot(a, b, preferred_element_type=F32)


def _dot_nt(a, b):
    return lax.dot_general(a, b, (((1,), (1,)), ((), ())), preferred_element_type=F32)


def _exact_dot_l(m01, x):
    hi, mid, lo = _split3(x)
    return _dot(m01, hi) + _dot(m01, mid) + _dot(m01, lo)


def _tril_bf16(n):
    r = lax.broadcasted_iota(jnp.int32, (n, n), 0)
    c = lax.broadcasted_iota(jnp.int32, (n, n), 1)
    return jnp.where(r >= c, 1.0, 0.0).astype(BF16)


def _inproj_kernel(x_ref, nw_ref, w_ref, ws_ref, p_ref, s_ref, h_ref):
    @pl.when(pl.program_id(1) == 0)
    def _():
        x = x_ref[...]
        ms = jnp.mean(x * x, axis=-1, keepdims=True)
        h = (x * lax.rsqrt(ms + EPS) * nw_ref[...]).astype(BF16)
        h_ref[...] = h
        s_ref[...] = _dot_nt(h, ws_ref[...])

    p_ref[...] = _dot_nt(h_ref[...], w_ref[...]).astype(BF16)


def _inproj(x2, norm_w, w_main_t, w_small_t):
    t = x2.shape[0]
    tm, tn = min(PROJ_TM, t), PROJ_TN
    return pl.pallas_call(
        _inproj_kernel,
        out_shape=(jax.ShapeDtypeStruct((t, N_MAIN), BF16),
                   jax.ShapeDtypeStruct((t, LANES), F32)),
        grid=(t // tm, N_MAIN // tn),
        in_specs=[
            pl.BlockSpec((tm, D_MODEL), lambda i, j: (i, 0)),
            pl.BlockSpec((1, D_MODEL), lambda i, j: (0, 0)),
            pl.BlockSpec((tn, D_MODEL), lambda i, j: (j, 0)),
            pl.BlockSpec((LANES, D_MODEL), lambda i, j: (0, 0)),
        ],
        out_specs=(pl.BlockSpec((tm, tn), lambda i, j: (i, j)),
                   pl.BlockSpec((tm, LANES), lambda i, j: (i, 0))),
        scratch_shapes=[pltpu.VMEM((tm, D_MODEL), BF16)],
        compiler_params=pltpu.CompilerParams(
            dimension_semantics=("arbitrary", "arbitrary"), vmem_limit_bytes=VMEM_LIMIT),
        name="inproj",
    )(x2, norm_w, w_main_t, w_small_t)


def _fprep_kernel(s_ref, bias_ref, ft_ref, f_scr, *, seq):
    tril = _tril_bf16(LANES)
    carry = jnp.zeros((1, LANES), F32)
    for blk in range(seq // LANES):
        x = s_ref[blk * LANES:(blk + 1) * LANES, :] + bias_ref[...]
        log_f = jnp.minimum(x, 0.0) - jnp.log1p(jnp.exp(-jnp.abs(x)))
        c = _exact_dot_l(tril, log_f) + carry
        f_scr[blk * LANES:(blk + 1) * LANES, :] = c
        carry = c[LANES - 1:LANES, :]
    ft_ref[...] = f_scr[...].T[0:N_SMALL, :]


def _fprep(small, bias_pad, bsz, seq):
    return pl.pallas_call(
        functools.partial(_fprep_kernel, seq=seq),
        out_shape=jax.ShapeDtypeStruct((bsz, N_SMALL, seq), F32),
        grid=(bsz,),
        in_specs=[pl.BlockSpec((seq, LANES), lambda b: (b, 0)),
                  pl.BlockSpec((1, LANES), lambda b: (0, 0))],
        out_specs=pl.BlockSpec((None, N_SMALL, seq), lambda b: (b, 0, 0)),
        scratch_shapes=[pltpu.VMEM((seq, LANES), F32)],
        compiler_params=pltpu.CompilerParams(
            dimension_semantics=("arbitrary",), vmem_limit_bytes=VMEM_LIMIT),
        name="fprep",
    )(small, bias_pad)


def _sublane_tiles(x, op):
    t = x[0:8, :]
    for r in range(1, x.shape[0] // 8):
        t = op(t, x[8 * r:8 * (r + 1), :])
    return t


def _scores(k_refs, qts, h, j, tile, mask):
    s = _dot(k_refs[h][j * tile:(j + 1) * tile, :], qts[h])
    return s if mask is None else jnp.where(mask, s, -jnp.inf)


def _two_pass_sweep(n_tiles, prepare, make_queries, k_refs, vt_refs, scr, tile, mask, finish,
                    early_exp):
    n = len(k_refs)
    bufs = lambda i: [scr[n * (i % 2) + h] for h in range(n)]

    def pass1(i):
        prepare(i)
        qts, ms = make_queries(i), [[] for _ in range(n)]
        for j in range(i + 1):
            for h in range(n):
                s = _scores(k_refs, qts, h, j, tile, mask if j == i else None)
                t = _sublane_tiles(s, jnp.maximum)
                if early_exp:
                    m = jnp.max(t, axis=0, keepdims=True)
                    m = jnp.maximum(ms[h][-1], m) if ms[h] else m
                    ms[h].append(m)
                    bufs(i)[h][j * tile:(j + 1) * tile, :] = jnp.exp2(s - m).astype(BF16)
                else:
                    ms[h] = [jnp.maximum(ms[h][0], t) if ms[h] else t]
                    bufs(i)[h][j * tile:(j + 1) * tile, :] = s
        if not early_exp:
            ms = [[jnp.max(t[0], axis=0, keepdims=True)] for t in ms]
        return ms

    def pass2(i, ms):
        accs = [None] * n
        for j in range(i + 1):
            for h in range(n):
                w = bufs(i)[h][j * tile:(j + 1) * tile, :]
                if not early_exp:
                    w = jnp.exp2(w - ms[h][0]).astype(BF16)
                pv = _dot(vt_refs[h][:, j * tile:(j + 1) * tile], w)
                if early_exp and j < i:
                    pv = pv * jnp.exp2(ms[h][j] - ms[h][i])
                accs[h] = pv if accs[h] is None else accs[h] + pv
        return accs

    ms = pass1(0)
    for i in range(n_tiles):
        ms_next = pass1(i + 1) if i + 1 < n_tiles else None
        finish(i, pass2(i, ms))
        ms = ms_next


def _fox_kernel(q_ref, k_ref, v_ref, g_ref, ft_ref, o_ref, k_scr, vt_scr, *s_scr, seq, tile):
    lane = lax.broadcasted_iota(jnp.int32, (1, LANES), 1)
    key = lax.broadcasted_iota(jnp.int32, (tile, tile), 0)
    qry = lax.broadcasted_iota(jnp.int32, (tile, tile), 1)
    causal = key <= qry
    own = (lane < HEAD_DIM, lane >= HEAD_DIM)
    bias_lo = (HEAD_DIM, 0)
    ones_q = tuple(jnp.logical_and(lane >= b, lane < b + 3).astype(F32) for b in bias_lo)
    streams = [(pr, hh) for pr in range(FOX_PAIRS) for hh in range(2)]

    def prepare(i):
        r0 = i * tile
        for pr in range(FOX_PAIRS):
            k = k_ref[r0:r0 + tile, pr * LANES:(pr + 1) * LANES]
            vt = v_ref[r0:r0 + tile, pr * LANES:(pr + 1) * LANES].astype(F32).T
            for hh in range(2):
                b = bias_lo[hh]
                hi, mid, lo = _split3(ft_ref[pr, hh:hh + 1, r0:r0 + tile] * (-LOG2E))
                rows = jnp.concatenate([hi.astype(F32), mid.astype(F32), lo.astype(F32),
                                        jnp.zeros((5, tile), F32)], axis=0)
                block = jnp.concatenate(([jnp.zeros((b, tile), F32)] if b else []) + [rows]
                                        + [jnp.zeros((LANES - b - 8, tile), F32)], axis=0)
                st = 2 * pr + hh
                k_scr[st, r0:r0 + tile, :] = jnp.where(own[hh], k, block.T.astype(BF16))
                vt_scr[st, :, r0:r0 + tile] = jnp.concatenate(
                    [vt[hh * HEAD_DIM:(hh + 1) * HEAD_DIM], jnp.ones((SUM_ROWS, tile), F32)],
                    axis=0).astype(BF16)

    def queries(i):
        out = []
        for pr, hh in streams:
            q = q_ref[i * tile:(i + 1) * tile, pr * LANES:(pr + 1) * LANES].astype(F32)
            out.append(jnp.where(own[hh], q, ones_q[hh]).T.astype(BF16))
        return out

    def finish(i, accs):
        r0 = i * tile
        for pr in range(FOX_PAIRS):
            o_t = jnp.concatenate(
                [a[0:HEAD_DIM] / a[HEAD_DIM:HEAD_DIM + 1] for a in accs[2 * pr:2 * pr + 2]], axis=0)
            g = g_ref[r0:r0 + tile, pr * LANES:(pr + 1) * LANES].astype(F32)
            o_ref[r0:r0 + tile, pr * LANES:(pr + 1) * LANES] = (o_t.T * _silu(g)).astype(BF16)

    n = len(streams)
    _two_pass_sweep(seq // tile, prepare, queries, [k_scr.at[st] for st in range(n)],
                    [vt_scr.at[st] for st in range(n)], s_scr, tile, causal, finish, True)


def _fox(p, ft4, bsz, seq):
    tile = min(ATT_TILE, seq)
    w = FOX_PAIRS * LANES
    blk = lambda off: pl.BlockSpec((seq, w), lambda b, h, off=off: (b, off // w + h))
    n = 2 * FOX_PAIRS
    return pl.pallas_call(
        functools.partial(_fox_kernel, seq=seq, tile=tile),
        out_shape=jax.ShapeDtypeStruct((bsz * seq, D_MODEL), BF16),
        grid=(bsz, D_MODEL // w),
        in_specs=[blk(C_FQ), blk(C_FK), blk(C_FV), blk(C_FG),
                  pl.BlockSpec((None, FOX_PAIRS, 2, seq), lambda b, h: (b, h, 0, 0))],
        out_specs=pl.BlockSpec((seq, w), lambda b, h: (b, h)),
        scratch_shapes=[pltpu.VMEM((n, seq, LANES), BF16),
                        pltpu.VMEM((n, HEAD_DIM + SUM_ROWS, seq), BF16)]
        + [pltpu.VMEM((seq, tile), BF16)] * (2 * n),
        compiler_params=pltpu.CompilerParams(
            dimension_semantics=("arbitrary", "arbitrary"), vmem_limit_bytes=VMEM_LIMIT),
        name="fox",
    )(p, p, p, p, ft4)


def _diff_kernel(q_ref, k_ref, v_ref, g_ref, cos_ref, s1_ref, s2_ref, lam_ref, sub_ref,
                 o_ref, k_scr, vt_scr, *s_scr, seq, tile, lam_init):
    def rope(x, r0):
        return (x * cos_ref[r0:r0 + tile, :]
                + pltpu.roll(x, LANES - ROT_HALF, 1) * s1_ref[r0:r0 + tile, :]
                + pltpu.roll(x, ROT_HALF, 1) * s2_ref[r0:r0 + tile, :])

    def prepare(i):
        r0 = i * tile
        for hd in range(DIFF_GROUP):
            cols = slice(hd * LANES, (hd + 1) * LANES)
            k_scr[hd, r0:r0 + tile, :] = rope(k_ref[r0:r0 + tile, cols].astype(F32), r0).astype(BF16)
            vt = v_ref[r0:r0 + tile, cols].astype(F32).T
            vt_scr[hd, :, r0:r0 + tile] = jnp.concatenate(
                [vt, jnp.ones((SUM_ROWS, tile), F32)], axis=0).astype(BF16)

    lp = lam_ref[...]
    lam = (jnp.exp(jnp.sum(lp[0:1] * lp[1:2], axis=1, keepdims=True))
           - jnp.exp(jnp.sum(lp[2:3] * lp[3:4], axis=1, keepdims=True)) + lam_init)

    lane = lax.broadcasted_iota(jnp.int32, (1, LANES), 1)
    first = lane < HEAD_DIM
    key = lax.broadcasted_iota(jnp.int32, (tile, tile), 0)
    qry = lax.broadcasted_iota(jnp.int32, (tile, tile), 1)
    chunk_causal = (key // MASK_CHUNK) <= (qry // MASK_CHUNK)

    def queries(i):
        r0 = i * tile
        out = []
        for hd in range(DIFF_GROUP):
            q = rope(q_ref[r0:r0 + tile, hd * LANES:(hd + 1) * LANES].astype(F32), r0)
            out += [jnp.where(first, q, 0.0).T.astype(BF16), jnp.where(first, 0.0, q).T.astype(BF16)]
        return out

    def finish(i, accs):
        r0 = i * tile
        for hd in range(DIFF_GROUP):
            a0, a1 = accs[2 * hd], accs[2 * hd + 1]
            o = (a0[0:LANES] / a0[LANES:LANES + 1] - lam * (a1[0:LANES] / a1[LANES:LANES + 1])).T
            ms = jnp.mean(o * o, axis=-1, keepdims=True)
            o = o * lax.rsqrt(ms + EPS) * sub_ref[...] * (1.0 - lam_init)
            g = g_ref[r0:r0 + tile, hd * LANES:(hd + 1) * LANES].astype(F32)
            o_ref[r0:r0 + tile, hd * LANES:(hd + 1) * LANES] = (o * _silu(g)).astype(BF16)

    heads = [hd for hd in range(DIFF_GROUP) for _ in range(2)]
    _two_pass_sweep(seq // tile, prepare, queries, [k_scr.at[hd] for hd in heads],
                    [vt_scr.at[hd] for hd in heads], s_scr, tile, chunk_causal, finish, False)


def _diff(p, cos_t, s1_t, s2_t, lam_p, sub_w, bsz, seq, lam_init):
    tile = min(ATT_TILE, seq)
    w = DIFF_GROUP * LANES
    blk = lambda off: pl.BlockSpec((seq, w), lambda b, h, off=off: (b, off // w + h))
    const = lambda shape: pl.BlockSpec(shape, lambda b, h: (0, 0))
    return pl.pallas_call(
        functools.partial(_diff_kernel, seq=seq, tile=tile, lam_init=lam_init),
        out_shape=jax.ShapeDtypeStruct((bsz * seq, D_MODEL), BF16),
        grid=(bsz, D_MODEL // w),
        in_specs=[blk(C_DQ), blk(C_DK), blk(C_DV), blk(C_DG),
                  const((seq, LANES)), const((seq, LANES)), const((seq, LANES)),
                  const((4, HEAD_DIM)), const((1, LANES))],
        out_specs=pl.BlockSpec((seq, w), lambda b, h: (b, h)),
        scratch_shapes=[pltpu.VMEM((DIFF_GROUP, seq, LANES), BF16),
                        pltpu.VMEM((DIFF_GROUP, LANES + SUM_ROWS, seq), BF16)]
        + [pltpu.VMEM((seq, tile), F32)] * (4 * DIFF_GROUP),
        compiler_params=pltpu.CompilerParams(
            dimension_semantics=("arbitrary", "arbitrary"), vmem_limit_bytes=VMEM_LIMIT),
        name="diffattn",
    )(p, p, p, p, cos_t, s1_t, s2_t, lam_p, sub_w)


def _ssd_kernel(xbc_ref, z_ref, sm_ref, cw_ref, cb_ref, dtb_ref, alog_ref, dskip_ref, nw_ref,
                e_ref, shift_ref, tril_ref, o_ref, u_scr, st_scr, *, q, nsub):
    @pl.when(pl.program_id(1) == 0)
    def _():
        u_scr[0:CONV_PAD, :] = jnp.zeros((CONV_PAD, 2 * D_MODEL), BF16)
        st_scr[...] = jnp.zeros_like(st_scr)

    rows = q * nsub
    u_scr[CONV_PAD:CONV_PAD + rows, :] = xbc_ref[...]
    lane = lax.broadcasted_iota(jnp.int32, (1, LANES), 1)
    is_head = jnp.logical_and(lane >= N_SMALL, lane < 2 * N_SMALL)
    a_neg = jnp.where(is_head, -jnp.exp(alog_ref[...]), 0.0)
    row = lax.broadcasted_iota(jnp.int32, (q, q), 0)
    col = lax.broadcasted_iota(jnp.int32, (q, q), 1)
    tril = col <= row
    first = lane < HEAD_DIM

    for sc in range(nsub):
        r0 = sc * q
        delayed = _dot(shift_ref[...], u_scr[r0:r0 + CONV_PAD + q, :])
        conv = cb_ref[...] + cw_ref[CONV_K - 1:CONV_K, :] * xbc_ref[r0:r0 + q, :].astype(F32)
        for kk in range(CONV_K - 1):
            conv = conv + cw_ref[kk:kk + 1, :] * delayed[kk * q:(kk + 1) * q, :]
        act = _silu(conv)
        xs = act[:, 0:D_MODEL]

        dt = _softplus(sm_ref[r0:r0 + q, :] + dtb_ref[...])
        a_cs = _dot(tril_ref[...], jnp.concatenate(_split3(dt * a_neg), axis=0))
        ea = jnp.exp(a_cs)
        eds = jnp.exp(a_cs[q - 1:q, :] - a_cs)
        stacked = jnp.concatenate([ea, eds * dt], axis=0)
        hi = stacked.astype(BF16)
        mid = (stacked - hi.astype(F32)).astype(BF16)
        wide = _dot(jnp.concatenate([hi, mid], axis=1), e_ref[...])
        ea_x, w_x = wide[0:q], wide[q:2 * q]
        a_cs_t = a_cs.T
        dt_t = dt.T

        xs_b = xs.astype(BF16)
        xdt_s = (xs * w_x).astype(BF16)

        ys = []
        for g in range(SSM_GROUPS):
            lo = g * SSM_GROUP_W
            bg = act[:, D_MODEL + g * SSM_STATE:D_MODEL + (g + 1) * SSM_STATE].astype(BF16)
            cg = act[:, D_MODEL + SSM_GROUPS * SSM_STATE + g * SSM_STATE:
                     D_MODEL + SSM_GROUPS * SSM_STATE + (g + 1) * SSM_STATE].astype(BF16)
            cb = _dot_nt(cg, bg)
            state = st_scr[g]
            y_off = _dot(cg, state.astype(BF16)) * ea_x[:, lo:lo + SSM_GROUP_W]
            y_diag = []
            for pp in range(2):
                xp = xs_b[:, lo + pp * LANES:lo + (pp + 1) * LANES]
                zero = jnp.zeros_like(xp)
                x_blocks = jnp.concatenate(
                    [jnp.where(first, xp, zero), jnp.where(first, zero, xp)], axis=0)
                mats = []
                for hh in range(2):
                    hl = N_SMALL + g * 4 + pp * 2 + hh
                    seg = a_cs[:, hl:hl + 1] - a_cs_t[hl:hl + 1, :]
                    decay = jnp.where(tril, jnp.exp(seg), 0.0) * dt_t[hl:hl + 1, :]
                    mats.append((cb * decay).astype(BF16))
                y_diag.append(_dot(jnp.concatenate(mats, axis=1), x_blocks))
            new_states = lax.dot_general(bg, xdt_s[:, lo:lo + SSM_GROUP_W],
                                         (((0,), (0,)), ((), ())), preferred_element_type=F32)
            st_scr[g] = state * ea_x[q - 1:q, lo:lo + SSM_GROUP_W] + new_states
            ys.append(jnp.concatenate(y_diag, axis=1) + y_off)

        y = jnp.concatenate(ys, axis=1) + xs * dskip_ref[...]
        z = z_ref[r0:r0 + q, :].astype(F32)
        yg = y * _silu(z)
        outs = []
        for g in range(SSM_GROUPS):
            v = yg[:, g * SSM_GROUP_W:(g + 1) * SSM_GROUP_W]
            ms = jnp.mean(v * v, axis=-1, keepdims=True)
            outs.append(v * lax.rsqrt(ms + EPS))
        o_ref[r0:r0 + q, :] = (jnp.concatenate(outs, axis=1) * nw_ref[...]).astype(BF16)

    u_scr[CONV_PAD - CONV_HALO:CONV_PAD, :] = xbc_ref[rows - CONV_HALO:rows, :]


def _conv_shift_matrix(q):
    m = np.zeros((3 * q, CONV_PAD + q), np.float32)
    for k in range(CONV_K - 1):
        t = np.arange(q)
        m[k * q + t, CONV_PAD + t - (CONV_K - 1) + k] = 1.0
    return jnp.asarray(m, BF16)


def _ssd(p, small, conv_w, conv_b, dtb_pad, alog_pad, dskip_x, norm_w, expand, bsz, seq):
    q = min(SSD_CHUNK, seq)
    nsub = SSD_SUBCHUNKS if seq % (q * SSD_SUBCHUNKS) == 0 else 1
    rows = q * nsub
    nc = seq // rows
    const = lambda shape: pl.BlockSpec(shape, lambda b, c: (0, 0))
    return pl.pallas_call(
        functools.partial(_ssd_kernel, q=q, nsub=nsub),
        out_shape=jax.ShapeDtypeStruct((bsz * seq, D_MODEL), BF16),
        grid=(bsz, nc),
        in_specs=[
            pl.BlockSpec((rows, 2 * D_MODEL), lambda b, c: (b * nc + c, C_XBC // (2 * D_MODEL))),
            pl.BlockSpec((rows, D_MODEL), lambda b, c: (b * nc + c, C_SZ // D_MODEL)),
            pl.BlockSpec((rows, LANES), lambda b, c: (b * nc + c, 0)),
            const((CONV_K, 2 * D_MODEL)), const((1, 2 * D_MODEL)),
            const((1, LANES)), const((1, LANES)),
            const((1, D_MODEL)), const((1, D_MODEL)),
            const((2 * LANES, D_MODEL)), const((3 * q, CONV_PAD + q)), const((q, 3 * q)),
        ],
        out_specs=pl.BlockSpec((rows, D_MODEL), lambda b, c: (b * nc + c, 0)),
        scratch_shapes=[pltpu.VMEM((CONV_PAD + rows, 2 * D_MODEL), BF16),
                        pltpu.VMEM((SSM_GROUPS, SSM_STATE, SSM_GROUP_W), F32)],
        compiler_params=pltpu.CompilerParams(
            dimension_semantics=("arbitrary", "arbitrary"), vmem_limit_bytes=VMEM_LIMIT),
        name="ssd",
    )(p, p, small, conv_w, conv_b, dtb_pad, alog_pad, dskip_x, norm_w, expand,
      _conv_shift_matrix(q), jnp.asarray(np.tile(np.tril(np.ones((q, q), np.float32)), (1, 3)), BF16))


def _merge_kernel(x_ref, ya_ref, yb_ref, yc_ref, g0_ref, g1_ref, g2_ref, wb_ref, wo_ref,
                  fw_ref, o_ref, *, final):
    merged = None
    for n, (y_ref, g_ref) in enumerate(((ya_ref, g0_ref), (yb_ref, g1_ref), (yc_ref, g2_ref))):
        term = _sigmoid(g_ref[...].astype(F32)) * _dot(y_ref[...], wb_ref[n])
        merged = term if merged is None else merged + term
    out = x_ref[...] + _dot(merged.astype(BF16), wo_ref[...])
    if final:
        ms = jnp.mean(out * out, axis=-1, keepdims=True)
        out = out * lax.rsqrt(ms + EPS) * fw_ref[...]
    o_ref[...] = out


def _merge(x2, y_a, y_b, y_c, p, w_branch, w_out, final_w, final):
    t = x2.shape[0]
    tm = min(MERGE_TM, t)
    rows = lambda: pl.BlockSpec((tm, D_MODEL), lambda i: (i, 0))
    gate = lambda n: pl.BlockSpec((tm, D_MODEL), lambda i, n=n: (i, C_MG // D_MODEL + n))
    return pl.pallas_call(
        functools.partial(_merge_kernel, final=final),
        out_shape=jax.ShapeDtypeStruct((t, D_MODEL), F32),
        grid=(t // tm,),
        in_specs=[rows(), rows(), rows(), rows(), gate(0), gate(1), gate(2),
                  pl.BlockSpec((3, D_MODEL, D_MODEL), lambda i: (0, 0, 0)),
                  pl.BlockSpec((D_MODEL, D_MODEL), lambda i: (0, 0)),
                  pl.BlockSpec((1, D_MODEL), lambda i: (0, 0))],
        out_specs=rows(),
        compiler_params=pltpu.CompilerParams(
            dimension_semantics=("arbitrary",), vmem_limit_bytes=VMEM_LIMIT),
        name="merge",
    )(x2, y_a, y_b, y_c, p, p, p, w_branch, w_out, final_w)


def _rope_tables(seq):
    pos = jnp.arange(seq, dtype=F32)
    inv_freq = ROPE_THETA ** (-jnp.arange(0, 2 * ROT_HALF, 2, dtype=F32) / (2 * ROT_HALF))
    ang = pos[:, None] * inv_freq[None, :]
    cos, sin = jnp.cos(ang), jnp.sin(ang)
    ones = jnp.ones((seq, HEAD_DIM - 2 * ROT_HALF), F32)
    zeros = jnp.zeros((seq, ROT_HALF), F32)
    rest = jnp.zeros((seq, HEAD_DIM - 2 * ROT_HALF), F32)
    cos_c = jnp.concatenate([cos, cos, ones], axis=1)
    s1_c = jnp.concatenate([-sin, zeros, rest], axis=1)
    s2_c = jnp.concatenate([zeros, sin, rest], axis=1)
    two = lambda t: jnp.concatenate([t, t], axis=1)
    return two(cos_c), two(s1_c), two(s2_c)


def _pad_lanes(v, start):
    return jnp.zeros((1, LANES), F32).at[0, start:start + v.shape[0]].set(v.astype(F32))


_SEGMENTS = ((0, C_FQ, 1024, True), (1024, C_FK, 1024, False), (2048, C_FV, 1024, False),
             (3088, C_FG, 1024, False), (5136, C_XBC, 2048, False), (4112, C_SZ, 1024, False),
             (7200, C_DQ, 1024, True), (8224, C_DK, 1024, False), (9248, C_DV, 1024, False),
             (10272, C_DG, 1024, False), (11296, C_MG, 3072, False))
_SRC_FORGET, _SRC_DT, _N_IN = 3072, 7184, 14368


def _wprep_kernel(wt_ref, wm_ref, ws_ref):
    scale = LOG2E * HEAD_DIM ** -0.5
    for src, dst, n, scaled in _SEGMENTS:
        x = wt_ref[src:src + n, :]
        wm_ref[dst:dst + n, :] = (x * scale if scaled else x).astype(BF16)
    ws_ref[...] = jnp.concatenate(
        [wt_ref[_SRC_FORGET:_SRC_FORGET + N_SMALL, :], wt_ref[_SRC_DT:_SRC_DT + N_SMALL, :],
         jnp.zeros((LANES - 2 * N_SMALL, wt_ref.shape[1]), F32)], axis=0).astype(BF16)


def _projection_weights(w_in_t, layer):
    return pl.pallas_call(
        _wprep_kernel,
        out_shape=(jax.ShapeDtypeStruct((N_MAIN, D_MODEL), BF16),
                   jax.ShapeDtypeStruct((LANES, D_MODEL), BF16)),
        grid=(D_MODEL // LANES,),
        in_specs=[pl.BlockSpec((None, _N_IN, LANES), lambda i: (layer, 0, i))],
        out_specs=(pl.BlockSpec((N_MAIN, LANES), lambda i: (0, i)),
                   pl.BlockSpec((LANES, LANES), lambda i: (0, i))),
        compiler_params=pltpu.CompilerParams(
            dimension_semantics=("arbitrary",), vmem_limit_bytes=VMEM_LIMIT),
        name="wprep",
    )(w_in_t)


def kernel(x, norm_w, w_in, b_forget, conv_w, conv_b, dt_bias, a_log, d_skip, ssm_norm_w,
           diff_lambda, subln_w, w_branch, w_out, final_norm_w):
    bsz, seq, _ = x.shape
    depth = norm_w.shape[0]
    cos_t, s1_t, s2_t = _rope_tables(seq)
    heads = jnp.arange(D_MODEL) // HEAD_DIM
    expand = (jnp.arange(2 * LANES)[:, None] % LANES == (heads[None, :] + N_SMALL)).astype(BF16)

    w_in_t = jnp.swapaxes(w_in, 1, 2)
    w_branch_b, w_out_b = w_branch.astype(BF16), w_out.astype(BF16)

    x2 = x.reshape(bsz * seq, D_MODEL)
    for layer in range(depth):
        w_main_t, w_small_t = _projection_weights(w_in_t, layer)
        p, small = _inproj(x2, norm_w[layer][None, :], w_main_t, w_small_t)

        ft = _fprep(small, _pad_lanes(b_forget[layer], 0), bsz, seq)
        y_a = _fox(p, ft.reshape(bsz, N_SMALL // 2, 2, seq), bsz, seq)

        y_b = _ssd(p, small, conv_w[layer], conv_b[layer][None, :],
                   _pad_lanes(dt_bias[layer], N_SMALL),
                   _pad_lanes(a_log[layer], N_SMALL),
                   jnp.repeat(d_skip[layer], HEAD_DIM)[None, :],
                   ssm_norm_w[layer][None, :], expand, bsz, seq)

        lam_init = 0.8 - 0.6 * math.exp(-0.3 * layer)
        y_c = _diff(p, cos_t, s1_t, s2_t, diff_lambda[layer], subln_w[layer][None, :],
                    bsz, seq, lam_init)

        x2 = _merge(x2, y_a, y_b, y_c, p, w_branch_b[layer], w_out_b[layer],
                    final_norm_w[None, :], layer == depth - 1)
    return x2.reshape(bsz, seq, D_MODEL)
```

```python
import functools
import math

import numpy as np
import jax
import jax.numpy as jnp
from jax import lax
from jax.experimental import pallas as pl
from jax.experimental.pallas import tpu as pltpu

F32 = jnp.float32
BF16 = jnp.bfloat16

D_MODEL = 1024
EPS = 1e-6
LOG2E = 1.4426950408889634
HEAD_DIM = 64
LANES = 128
N_SMALL = 16
SSM_GROUPS = 4
SSM_STATE = 128
SSM_GROUP_W = 256
CONV_K = 4
CONV_HALO = 16
CONV_PAD = 128
DIFF_HEADS = 8
ROT_HALF = 8
ROPE_THETA = 500000.0
MASK_CHUNK = 64
SUM_ROWS = 16

C_FQ, C_FK, C_FV, C_FG = 0, 1024, 2048, 3072
C_XBC, C_SZ = 4096, 6144
C_DQ, C_DK, C_DV, C_DG = 7168, 8192, 9216, 10240
C_MG = 11264
N_MAIN = 14336

ATT_TILE = 256
FOX_PAIRS = 2
DIFF_GROUP = 2
SSD_CHUNK = 128
SSD_SUBCHUNKS = 4
PROJ_TM, PROJ_TN = 2048, 1024
MERGE_TM = 512
VMEM_LIMIT = 56 * 1024 * 1024


def _sigmoid(x):
    return 0.5 * jnp.tanh(0.5 * x) + 0.5


def _silu(x):
    h = 0.5 * x
    return h + h * jnp.tanh(h)


def _softplus(x):
    return jnp.maximum(x, 0.0) + jnp.log1p(jnp.exp(-jnp.abs(x)))


def _split3(x):
    hi = x.astype(BF16)
    r1 = x - hi.astype(F32)
    mid = r1.astype(BF16)
    lo = (r1 - mid.astype(F32)).astype(BF16)
    return hi, mid, lo


def _dot(a, b):
    return jnp.dot(a, b, preferred_element_type=F32)


def _dot_nt(a, b):
    return lax.dot_general(a, b, (((1,), (1,)), ((), ())), preferred_element_type=F32)


def _exact_dot_l(m01, x):
    hi, mid, lo = _split3(x)
    return _dot(m01, hi) + _dot(m01, mid) + _dot(m01, lo)


def _tril_bf16(n):
    r = lax.broadcasted_iota(jnp.int32, (n, n), 0)
    c = lax.broadcasted_iota(jnp.int32, (n, n), 1)
    return jnp.where(r >= c, 1.0, 0.0).astype(BF16)


def _inproj_kernel(x_ref, nw_ref, w_ref, ws_ref, p_ref, s_ref, h_ref):
    @pl.when(pl.program_id(1) == 0)
    def _():
        x = x_ref[...]
        ms = jnp.mean(x * x, axis=-1, keepdims=True)
        h = (x * lax.rsqrt(ms + EPS) * nw_ref[...]).astype(BF16)
        h_ref[...] = h
        s_ref[...] = _dot_nt(h, ws_ref[...])

    p_ref[...] = _dot_nt(h_ref[...], w_ref[...]).astype(BF16)


def _inproj(x2, norm_w, w_main_t, w_small_t):
    t = x2.shape[0]
    tm, tn = min(PROJ_TM, t), PROJ_TN
    return pl.pallas_call(
        _inproj_kernel,
        out_shape=(jax.ShapeDtypeStruct((t, N_MAIN), BF16),
                   jax.ShapeDtypeStruct((t, LANES), F32)),
        grid=(t // tm, N_MAIN // tn),
        in_specs=[
            pl.BlockSpec((tm, D_MODEL), lambda i, j: (i, 0)),
            pl.BlockSpec((1, D_MODEL), lambda i, j: (0, 0)),
            pl.BlockSpec((tn, D_MODEL), lambda i, j: (j, 0)),
            pl.BlockSpec((LANES, D_MODEL), lambda i, j: (0, 0)),
        ],
        out_specs=(pl.BlockSpec((tm, tn), lambda i, j: (i, j)),
                   pl.BlockSpec((tm, LANES), lambda i, j: (i, 0))),
        scratch_shapes=[pltpu.VMEM((tm, D_MODEL), BF16)],
        compiler_params=pltpu.CompilerParams(
            dimension_semantics=("arbitrary", "arbitrary"), vmem_limit_bytes=VMEM_LIMIT),
        name="inproj",
    )(x2, norm_w, w_main_t, w_small_t)


def _fprep_kernel(s_ref, bias_ref, ft_ref, f_scr, *, seq):
    tril = _tril_bf16(LANES)
    carry = jnp.zeros((1, LANES), F32)
    for blk in range(seq // LANES):
        x = s_ref[blk * LANES:(blk + 1) * LANES, :] + bias_ref[...]
        log_f = jnp.minimum(x, 0.0) - jnp.log1p(jnp.exp(-jnp.abs(x)))
        c = _exact_dot_l(tril, log_f) + carry
        f_scr[blk * LANES:(blk + 1) * LANES, :] = c
        carry = c[LANES - 1:LANES, :]
    ft_ref[...] = f_scr[...].T[0:N_SMALL, :]


def _fprep(small, bias_pad, bsz, seq):
    return pl.pallas_call(
        functools.partial(_fprep_kernel, seq=seq),
        out_shape=jax.ShapeDtypeStruct((bsz, N_SMALL, seq), F32),
        grid=(bsz,),
        in_specs=[pl.BlockSpec((seq, LANES), lambda b: (b, 0)),
                  pl.BlockSpec((1, LANES), lambda b: (0, 0))],
        out_specs=pl.BlockSpec((None, N_SMALL, seq), lambda b: (b, 0, 0)),
        scratch_shapes=[pltpu.VMEM((seq, LANES), F32)],
        compiler_params=pltpu.CompilerParams(
            dimension_semantics=("arbitrary",), vmem_limit_bytes=VMEM_LIMIT),
        name="fprep",
    )(small, bias_pad)


def _sublane_tiles(x, op):
    t = x[0:8, :]
    for r in range(1, x.shape[0] // 8):
        t = op(t, x[8 * r:8 * (r + 1), :])
    return t


def _scores(k_refs, qts, h, j, tile, mask):
    s = _dot(k_refs[h][j * tile:(j + 1) * tile, :], qts[h])
    return s if mask is None else jnp.where(mask, s, -jnp.inf)


def _two_pass_sweep(n_tiles, prepare, make_queries, k_refs, vt_refs, scr, tile, mask, finish,
                    early_exp):
    n = len(k_refs)
    bufs = lambda i: [scr[n * (i % 2) + h] for h in range(n)]

    def pass1(i):
        prepare(i)
        qts, ms = make_queries(i), [[] for _ in range(n)]
        for h in range(n):
            for j in range(i + 1):
                s = _scores(k_refs, qts, h, j, tile, mask if j == i else None)
                t = _sublane_tiles(s, jnp.maximum)
                if early_exp:
                    m = jnp.max(t, axis=0, keepdims=True)
                    m = jnp.maximum(ms[h][-1], m) if ms[h] else m
                    ms[h].append(m)
                    bufs(i)[h][j * tile:(j + 1) * tile, :] = jnp.exp2(s - m).astype(BF16)
                else:
                    ms[h] = [jnp.maximum(ms[h][0], t) if ms[h] else t]
                    bufs(i)[h][j * tile:(j + 1) * tile, :] = s
        if not early_exp:
            ms = [[jnp.max(t[0], axis=0, keepdims=True)] for t in ms]
        return ms

    def pass2(i, ms):
        accs = [None] * n
        for h in range(n):
            for j in range(i + 1):
                w = bufs(i)[h][j * tile:(j + 1) * tile, :]
                if not early_exp:
                    w = jnp.exp2(w - ms[h][0]).astype(BF16)
                pv = _dot(vt_refs[h][:, j * tile:(j + 1) * tile], w)
                if early_exp and j < i:
                    pv = pv * jnp.exp2(ms[h][j] - ms[h][i])
                accs[h] = pv if accs[h] is None else accs[h] + pv
        return accs

    ms = pass1(0)
    for i in range(n_tiles):
        ms_next = pass1(i + 1) if i + 1 < n_tiles else None
        finish(i, pass2(i, ms))
        ms = ms_next


def _fox_kernel(q_ref, k_ref, v_ref, g_ref, ft_ref, o_ref, k_scr, vt_scr, *s_scr, seq, tile):
    lane = lax.broadcasted_iota(jnp.int32, (1, LANES), 1)
    sub = lax.broadcasted_iota(jnp.int32, (LANES, 1), 0)
    key = lax.broadcasted_iota(jnp.int32, (tile, tile), 0)
    qry = lax.broadcasted_iota(jnp.int32, (tile, tile), 1)
    causal = key <= qry
    own = (lane < HEAD_DIM, lane >= HEAD_DIM)
    own_t = (sub < HEAD_DIM, sub >= HEAD_DIM)
    bias_lo = (HEAD_DIM, 0)
    ones_q = tuple(jnp.logical_and(lane >= b, lane < b + 3).astype(F32) for b in bias_lo)
    streams = [(pr, hh) for pr in range(FOX_PAIRS) for hh in range(2)]

    def prepare(i):
        r0 = i * tile
        for pr in range(FOX_PAIRS):
            k = k_ref[r0:r0 + tile, pr * LANES:(pr + 1) * LANES]
            vt = v_ref[r0:r0 + tile, pr * LANES:(pr + 1) * LANES].astype(F32).T
            for hh in range(2):
                b = bias_lo[hh]
                hi, mid, lo = _split3(ft_ref[pr, hh:hh + 1, r0:r0 + tile] * (-LOG2E))
                rows = jnp.concatenate([hi.astype(F32), mid.astype(F32), lo.astype(F32),
                                        jnp.zeros((5, tile), F32)], axis=0)
                block = jnp.concatenate(([jnp.zeros((b, tile), F32)] if b else []) + [rows]
                                        + [jnp.zeros((LANES - b - 8, tile), F32)], axis=0)
                st = 2 * pr + hh
                k_scr[st, r0:r0 + tile, :] = jnp.where(own[hh], k, block.T.astype(BF16))
                vt_scr[st, :, r0:r0 + tile] = jnp.where(own_t[hh], vt, 1.0).astype(BF16)

    def queries(i):
        out = []
        for pr, hh in streams:
            q = q_ref[i * tile:(i + 1) * tile, pr * LANES:(pr + 1) * LANES].astype(F32)
            out.append(jnp.where(own[hh], q, ones_q[hh]).T.astype(BF16))
        return out

    def finish(i, accs):
        r0 = i * tile
        for pr in range(FOX_PAIRS):
            a0, a1 = accs[2 * pr], accs[2 * pr + 1]
            num = jnp.where(own_t[0], a0, a1)
            den = jnp.where(own_t[0], a0[HEAD_DIM:HEAD_DIM + 1, :], a1[0:1, :])
            g = g_ref[r0:r0 + tile, pr * LANES:(pr + 1) * LANES].astype(F32)
            o_ref[r0:r0 + tile, pr * LANES:(pr + 1) * LANES] = (
                (num / den).T * _silu(g)).astype(BF16)

    n = len(streams)
    _two_pass_sweep(seq // tile, prepare, queries, [k_scr.at[st] for st in range(n)],
                    [vt_scr.at[st] for st in range(n)], s_scr, tile, causal, finish, True)


def _fox(p, ft4, bsz, seq):
    tile = min(ATT_TILE, seq)
    w = FOX_PAIRS * LANES
    blk = lambda off: pl.BlockSpec((seq, w), lambda b, h, off=off: (b, off // w + h))
    n = 2 * FOX_PAIRS
    return pl.pallas_call(
        functools.partial(_fox_kernel, seq=seq, tile=tile),
        out_shape=jax.ShapeDtypeStruct((bsz * seq, D_MODEL), BF16),
        grid=(bsz, D_MODEL // w),
        in_specs=[blk(C_FQ), blk(C_FK), blk(C_FV), blk(C_FG),
                  pl.BlockSpec((None, FOX_PAIRS, 2, seq), lambda b, h: (b, h, 0, 0))],
        out_specs=pl.BlockSpec((seq, w), lambda b, h: (b, h)),
        scratch_shapes=[pltpu.VMEM((n, seq, LANES), BF16), pltpu.VMEM((n, LANES, seq), BF16)]
        + [pltpu.VMEM((seq, tile), BF16)] * (2 * n),
        compiler_params=pltpu.CompilerParams(
            dimension_semantics=("arbitrary", "arbitrary"), vmem_limit_bytes=VMEM_LIMIT),
        name="fox",
    )(p, p, p, p, ft4)


def _diff_kernel(q_ref, k_ref, v_ref, g_ref, cos_ref, s1_ref, s2_ref, lam_ref, sub_ref,
                 o_ref, k_scr, vt_scr, *s_scr, seq, tile, lam_init):
    def rope(x, r0):
        return (x * cos_ref[r0:r0 + tile, :]
                + pltpu.roll(x, LANES - ROT_HALF, 1) * s1_ref[r0:r0 + tile, :]
                + pltpu.roll(x, ROT_HALF, 1) * s2_ref[r0:r0 + tile, :])

    def prepare(i):
        r0 = i * tile
        for hd in range(DIFF_GROUP):
            cols = slice(hd * LANES, (hd + 1) * LANES)
            k_scr[hd, r0:r0 + tile, :] = rope(k_ref[r0:r0 + tile, cols].astype(F32), r0).astype(BF16)
            vt = v_ref[r0:r0 + tile, cols].astype(F32).T
            vt_scr[hd, :, r0:r0 + tile] = jnp.concatenate(
                [vt, jnp.ones((SUM_ROWS, tile), F32)], axis=0).astype(BF16)

    lp = lam_ref[...]
    lam = (jnp.exp(jnp.sum(lp[0:1] * lp[1:2], axis=1, keepdims=True))
           - jnp.exp(jnp.sum(lp[2:3] * lp[3:4], axis=1, keepdims=True)) + lam_init)

    lane = lax.broadcasted_iota(jnp.int32, (1, LANES), 1)
    first = lane < HEAD_DIM
    key = lax.broadcasted_iota(jnp.int32, (tile, tile), 0)
    qry = lax.broadcasted_iota(jnp.int32, (tile, tile), 1)
    chunk_causal = (key // MASK_CHUNK) <= (qry // MASK_CHUNK)

    def queries(i):
        r0 = i * tile
        out = []
        for hd in range(DIFF_GROUP):
            q = rope(q_ref[r0:r0 + tile, hd * LANES:(hd + 1) * LANES].astype(F32), r0)
            out += [jnp.where(first, q, 0.0).T.astype(BF16), jnp.where(first, 0.0, q).T.astype(BF16)]
        return out

    def finish(i, accs):
        r0 = i * tile
        for hd in range(DIFF_GROUP):
            a0, a1 = accs[2 * hd], accs[2 * hd + 1]
            o = (a0[0:LANES] / a0[LANES:LANES + 1] - lam * (a1[0:LANES] / a1[LANES:LANES + 1])).T
            ms = jnp.mean(o * o, axis=-1, keepdims=True)
            o = o * lax.rsqrt(ms + EPS) * sub_ref[...] * (1.0 - lam_init)
            g = g_ref[r0:r0 + tile, hd * LANES:(hd + 1) * LANES].astype(F32)
            o_ref[r0:r0 + tile, hd * LANES:(hd + 1) * LANES] = (o * _silu(g)).astype(BF16)

    heads = [hd for hd in range(DIFF_GROUP) for _ in range(2)]
    _two_pass_sweep(seq // tile, prepare, queries, [k_scr.at[hd] for hd in heads],
                    [vt_scr.at[hd] for hd in heads], s_scr, tile, chunk_causal, finish, False)


def _diff(p, cos_t, s1_t, s2_t, lam_p, sub_w, bsz, seq, lam_init):
    tile = min(ATT_TILE, seq)
    w = DIFF_GROUP * LANES
    blk = lambda off: pl.BlockSpec((seq, w), lambda b, h, off=off: (b, off // w + h))
    const = lambda shape: pl.BlockSpec(shape, lambda b, h: (0, 0))
    return pl.pallas_call(
        functools.partial(_diff_kernel, seq=seq, tile=tile, lam_init=lam_init),
        out_shape=jax.ShapeDtypeStruct((bsz * seq, D_MODEL), BF16),
        grid=(bsz, D_MODEL // w),
        in_specs=[blk(C_DQ), blk(C_DK), blk(C_DV), blk(C_DG),
                  const((seq, LANES)), const((seq, LANES)), const((seq, LANES)),
                  const((4, HEAD_DIM)), const((1, LANES))],
        out_specs=pl.BlockSpec((seq, w), lambda b, h: (b, h)),
        scratch_shapes=[pltpu.VMEM((DIFF_GROUP, seq, LANES), BF16),
                        pltpu.VMEM((DIFF_GROUP, LANES + SUM_ROWS, seq), BF16)]
        + [pltpu.VMEM((seq, tile), F32)] * (4 * DIFF_GROUP),
        compiler_params=pltpu.CompilerParams(
            dimension_semantics=("arbitrary", "arbitrary"), vmem_limit_bytes=VMEM_LIMIT),
        name="diffattn",
    )(p, p, p, p, cos_t, s1_t, s2_t, lam_p, sub_w)


def _ssd_kernel(xbc_ref, z_ref, sm_ref, cw_ref, cb_ref, dtb_ref, alog_ref, dskip_ref, nw_ref,
                e_ref, shift_ref, tril_ref, o_ref, u_scr, st_scr, *, q, nsub):
    @pl.when(pl.program_id(1) == 0)
    def _():
        u_scr[0:CONV_PAD, :] = jnp.zeros((CONV_PAD, 2 * D_MODEL), BF16)
        st_scr[...] = jnp.zeros_like(st_scr)

    rows = q * nsub
    u_scr[CONV_PAD:CONV_PAD + rows, :] = xbc_ref[...]
    lane = lax.broadcasted_iota(jnp.int32, (1, LANES), 1)
    is_head = jnp.logical_and(lane >= N_SMALL, lane < 2 * N_SMALL)
    a_neg = jnp.where(is_head, -jnp.exp(alog_ref[...]), 0.0)
    row = lax.broadcasted_iota(jnp.int32, (q, q), 0)
    col = lax.broadcasted_iota(jnp.int32, (q, q), 1)
    tril = col <= row
    first = lane < HEAD_DIM

    for sc in range(nsub):
        r0 = sc * q
        delayed = _dot(shift_ref[...], u_scr[r0:r0 + CONV_PAD + q, :])
        conv = cb_ref[...] + cw_ref[CONV_K - 1:CONV_K, :] * xbc_ref[r0:r0 + q, :].astype(F32)
        for kk in range(CONV_K - 1):
            conv = conv + cw_ref[kk:kk + 1, :] * delayed[kk * q:(kk + 1) * q, :]
        act = _silu(conv)
        xs = act[:, 0:D_MODEL]

        dt = _softplus(sm_ref[r0:r0 + q, :] + dtb_ref[...])
        a_cs = _dot(tril_ref[...], jnp.concatenate(_split3(dt * a_neg), axis=0))
        ea = jnp.exp(a_cs)
        eds = jnp.exp(a_cs[q - 1:q, :] - a_cs)
        stacked = jnp.concatenate([ea, eds * dt], axis=0)
        hi = stacked.astype(BF16)
        mid = (stacked - hi.astype(F32)).astype(BF16)
        wide = _dot(jnp.concatenate([hi, mid], axis=1), e_ref[...])
        ea_x, w_x = wide[0:q], wide[q:2 * q]
        a_cs_t = a_cs.T
        dt_t = dt.T

        xs_b = xs.astype(BF16)
        xdt_s = (xs * w_x).astype(BF16)

        ys = []
        for g in range(SSM_GROUPS):
            lo = g * SSM_GROUP_W
            bg = act[:, D_MODEL + g * SSM_STATE:D_MODEL + (g + 1) * SSM_STATE].astype(BF16)
            cg = act[:, D_MODEL + SSM_GROUPS * SSM_STATE + g * SSM_STATE:
                     D_MODEL + SSM_GROUPS * SSM_STATE + (g + 1) * SSM_STATE].astype(BF16)
            cb = _dot_nt(cg, bg)
            state = st_scr[g]
            y_off = _dot(cg, state.astype(BF16)) * ea_x[:, lo:lo + SSM_GROUP_W]
            y_diag = []
            for pp in range(2):
                xp = xs_b[:, lo + pp * LANES:lo + (pp + 1) * LANES]
                zero = jnp.zeros_like(xp)
                x_blocks = jnp.concatenate(
                    [jnp.where(first, xp, zero), jnp.where(first, zero, xp)], axis=0)
                mats = []
                for hh in range(2):
                    hl = N_SMALL + g * 4 + pp * 2 + hh
                    seg = a_cs[:, hl:hl + 1] - a_cs_t[hl:hl + 1, :]
                    decay = jnp.where(tril, jnp.exp(seg), 0.0) * dt_t[hl:hl + 1, :]
                    mats.append((cb * decay).astype(BF16))
                y_diag.append(_dot(jnp.concatenate(mats, axis=1), x_blocks))
            new_states = lax.dot_general(bg, xdt_s[:, lo:lo + SSM_GROUP_W],
                                         (((0,), (0,)), ((), ())), preferred_element_type=F32)
            st_scr[g] = state * ea_x[q - 1:q, lo:lo + SSM_GROUP_W] + new_states
            ys.append(jnp.concatenate(y_diag, axis=1) + y_off)

        y = jnp.concatenate(ys, axis=1) + xs * dskip_ref[...]
        z = z_ref[r0:r0 + q, :].astype(F32)
        yg = y * _silu(z)
        outs = []
        for g in range(SSM_GROUPS):
            v = yg[:, g * SSM_GROUP_W:(g + 1) * SSM_GROUP_W]
            ms = jnp.mean(v * v, axis=-1, keepdims=True)
            outs.append(v * lax.rsqrt(ms + EPS))
        o_ref[r0:r0 + q, :] = (jnp.concatenate(outs, axis=1) * nw_ref[...]).astype(BF16)

    u_scr[CONV_PAD - CONV_HALO:CONV_PAD, :] = xbc_ref[rows - CONV_HALO:rows, :]


def _conv_shift_matrix(q):
    m = np.zeros((3 * q, CONV_PAD + q), np.float32)
    for k in range(CONV_K - 1):
        t = np.arange(q)
        m[k * q + t, CONV_PAD + t - (CONV_K - 1) + k] = 1.0
    return jnp.asarray(m, BF16)


def _ssd(p, small, conv_w, conv_b, dtb_pad, alog_pad, dskip_x, norm_w, expand, bsz, seq):
    q = min(SSD_CHUNK, seq)
    nsub = SSD_SUBCHUNKS if seq % (q * SSD_SUBCHUNKS) == 0 else 1
    rows = q * nsub
    nc = seq // rows
    const = lambda shape: pl.BlockSpec(shape, lambda b, c: (0, 0))
    return pl.pallas_call(
        functools.partial(_ssd_kernel, q=q, nsub=nsub),
        out_shape=jax.ShapeDtypeStruct((bsz * seq, D_MODEL), BF16),
        grid=(bsz, nc),
        in_specs=[
            pl.BlockSpec((rows, 2 * D_MODEL), lambda b, c: (b * nc + c, C_XBC // (2 * D_MODEL))),
            pl.BlockSpec((rows, D_MODEL), lambda b, c: (b * nc + c, C_SZ // D_MODEL)),
            pl.BlockSpec((rows, LANES), lambda b, c: (b * nc + c, 0)),
            const((CONV_K, 2 * D_MODEL)), const((1, 2 * D_MODEL)),
            const((1, LANES)), const((1, LANES)),
            const((1, D_MODEL)), const((1, D_MODEL)),
            const((2 * LANES, D_MODEL)), const((3 * q, CONV_PAD + q)), const((q, 3 * q)),
        ],
        out_specs=pl.BlockSpec((rows, D_MODEL), lambda b, c: (b * nc + c, 0)),
        scratch_shapes=[pltpu.VMEM((CONV_PAD + rows, 2 * D_MODEL), BF16),
                        pltpu.VMEM((SSM_GROUPS, SSM_STATE, SSM_GROUP_W), F32)],
        compiler_params=pltpu.CompilerParams(
            dimension_semantics=("arbitrary", "arbitrary"), vmem_limit_bytes=VMEM_LIMIT),
        name="ssd",
    )(p, p, small, conv_w, conv_b, dtb_pad, alog_pad, dskip_x, norm_w, expand,
      _conv_shift_matrix(q), jnp.asarray(np.tile(np.tril(np.ones((q, q), np.float32)), (1, 3)), BF16))


def _merge_kernel(x_ref, ya_ref, yb_ref, yc_ref, g0_ref, g1_ref, g2_ref, wb_ref, wo_ref,
                  fw_ref, o_ref, *, final):
    merged = None
    for n, (y_ref, g_ref) in enumerate(((ya_ref, g0_ref), (yb_ref, g1_ref), (yc_ref, g2_ref))):
        term = _sigmoid(g_ref[...].astype(F32)) * _dot(y_ref[...], wb_ref[n])
        merged = term if merged is None else merged + term
    out = x_ref[...] + _dot(merged.astype(BF16), wo_ref[...])
    if final:
        ms = jnp.mean(out * out, axis=-1, keepdims=True)
        out = out * lax.rsqrt(ms + EPS) * fw_ref[...]
    o_ref[...] = out


def _merge(x2, y_a, y_b, y_c, p, w_branch, w_out, final_w, final):
    t = x2.shape[0]
    tm = min(MERGE_TM, t)
    rows = lambda: pl.BlockSpec((tm, D_MODEL), lambda i: (i, 0))
    gate = lambda n: pl.BlockSpec((tm, D_MODEL), lambda i, n=n: (i, C_MG // D_MODEL + n))
    return pl.pallas_call(
        functools.partial(_merge_kernel, final=final),
        out_shape=jax.ShapeDtypeStruct((t, D_MODEL), F32),
        grid=(t // tm,),
        in_specs=[rows(), rows(), rows(), rows(), gate(0), gate(1), gate(2),
                  pl.BlockSpec((3, D_MODEL, D_MODEL), lambda i: (0, 0, 0)),
                  pl.BlockSpec((D_MODEL, D_MODEL), lambda i: (0, 0)),
                  pl.BlockSpec((1, D_MODEL), lambda i: (0, 0))],
        out_specs=rows(),
        compiler_params=pltpu.CompilerParams(
            dimension_semantics=("arbitrary",), vmem_limit_bytes=VMEM_LIMIT),
        name="merge",
    )(x2, y_a, y_b, y_c, p, p, p, w_branch, w_out, final_w)


def _rope_tables(seq):
    pos = jnp.arange(seq, dtype=F32)
    inv_freq = ROPE_THETA ** (-jnp.arange(0, 2 * ROT_HALF, 2, dtype=F32) / (2 * ROT_HALF))
    ang = pos[:, None] * inv_freq[None, :]
    cos, sin = jnp.cos(ang), jnp.sin(ang)
    ones = jnp.ones((seq, HEAD_DIM - 2 * ROT_HALF), F32)
    zeros = jnp.zeros((seq, ROT_HALF), F32)
    rest = jnp.zeros((seq, HEAD_DIM - 2 * ROT_HALF), F32)
    cos_c = jnp.concatenate([cos, cos, ones], axis=1)
    s1_c = jnp.concatenate([-sin, zeros, rest], axis=1)
    s2_c = jnp.concatenate([zeros, sin, rest], axis=1)
    two = lambda t: jnp.concatenate([t, t], axis=1)
    return two(cos_c), two(s1_c), two(s2_c)


def _pad_lanes(v, start):
    return jnp.zeros((1, LANES), F32).at[0, start:start + v.shape[0]].set(v.astype(F32))


_SEGMENTS = ((0, C_FQ, 1024, True), (1024, C_FK, 1024, False), (2048, C_FV, 1024, False),
             (3088, C_FG, 1024, False), (5136, C_XBC, 2048, False), (4112, C_SZ, 1024, False),
             (7200, C_DQ, 1024, True), (8224, C_DK, 1024, False), (9248, C_DV, 1024, False),
             (10272, C_DG, 1024, False), (11296, C_MG, 3072, False))
_SRC_FORGET, _SRC_DT, _N_IN = 3072, 7184, 14368


def _wprep_kernel(wt_ref, wm_ref, ws_ref):
    scale = LOG2E * HEAD_DIM ** -0.5
    for src, dst, n, scaled in _SEGMENTS:
        x = wt_ref[src:src + n, :]
        wm_ref[dst:dst + n, :] = (x * scale if scaled else x).astype(BF16)
    ws_ref[...] = jnp.concatenate(
        [wt_ref[_SRC_FORGET:_SRC_FORGET + N_SMALL, :], wt_ref[_SRC_DT:_SRC_DT + N_SMALL, :],
         jnp.zeros((LANES - 2 * N_SMALL, wt_ref.shape[1]), F32)], axis=0).astype(BF16)


def _projection_weights(w_in_t, layer):
    return pl.pallas_call(
        _wprep_kernel,
        out_shape=(jax.ShapeDtypeStruct((N_MAIN, D_MODEL), BF16),
                   jax.ShapeDtypeStruct((LANES, D_MODEL), BF16)),
        grid=(D_MODEL // LANES,),
        in_specs=[pl.BlockSpec((None, _N_IN, LANES), lambda i: (layer, 0, i))],
        out_specs=(pl.BlockSpec((N_MAIN, LANES), lambda i: (0, i)),
                   pl.BlockSpec((LANES, LANES), lambda i: (0, i))),
        compiler_params=pltpu.CompilerParams(
            dimension_semantics=("arbitrary",), vmem_limit_bytes=VMEM_LIMIT),
        name="wprep",
    )(w_in_t)


def kernel(x, norm_w, w_in, b_forget, conv_w, conv_b, dt_bias, a_log, d_skip, ssm_norm_w,
           diff_lambda, subln_w, w_branch, w_out, final_norm_w):
    bsz, seq, _ = x.shape
    depth = norm_w.shape[0]
    cos_t, s1_t, s2_t = _rope_tables(seq)
    heads = jnp.arange(D_MODEL) // HEAD_DIM
    expand = (jnp.arange(2 * LANES)[:, None] % LANES == (heads[None, :] + N_SMALL)).astype(BF16)

    w_in_t = jnp.swapaxes(w_in, 1, 2)
    w_branch_b, w_out_b = w_branch.astype(BF16), w_out.astype(BF16)

    x2 = x.reshape(bsz * seq, D_MODEL)
    for layer in range(depth):
        w_main_t, w_small_t = _projection_weights(w_in_t, layer)
        p, small = _inproj(x2, norm_w[layer][None, :], w_main_t, w_small_t)

        ft = _fprep(small, _pad_lanes(b_forget[layer], 0), bsz, seq)
        y_a = _fox(p, ft.reshape(bsz, N_SMALL // 2, 2, seq), bsz, seq)

        y_b = _ssd(p, small, conv_w[layer], conv_b[layer][None, :],
                   _pad_lanes(dt_bias[layer], N_SMALL),
                   _pad_lanes(a_log[layer], N_SMALL),
                   jnp.repeat(d_skip[layer], HEAD_DIM)[None, :],
                   ssm_norm_w[layer][None, :], expand, bsz, seq)

        lam_init = 0.8 - 0.6 * math.exp(-0.3 * layer)
        y_c = _diff(p, cos_t, s1_t, s2_t, diff_lambda[layer], subln_w[layer][None, :],
                    bsz, seq, lam_init)

        x2 = _merge(x2, y_a, y_b, y_c, p, w_branch_b[layer], w_out_b[layer],
                    final_norm_w[None, :], layer == depth - 1)
    return x2.reshape(bsz, seq, D_MODEL)
```

```python
import functools
import math

import numpy as np
import jax
import jax.numpy as jnp
from jax import lax
from jax.experimental import pallas as pl
from jax.experimental.pallas import tpu as pltpu

F32 = jnp.float32
BF16 = jnp.bfloat16

D_MODEL = 1024
EPS = 1e-6
LOG2E = 1.4426950408889634
HEAD_DIM = 64
LANES = 128
N_SMALL = 16
SSM_GROUPS = 4
SSM_STATE = 128
SSM_GROUP_W = 256
CONV_K = 4
CONV_HALO = 16
CONV_PAD = 128
DIFF_HEADS = 8
ROT_HALF = 8
ROPE_THETA = 500000.0
MASK_CHUNK = 64
SUM_ROWS = 16

C_FQ, C_FK, C_FV, C_FG = 0, 1024, 2048, 3072
C_XBC, C_SZ = 4096, 6144
C_DQ, C_DK, C_DV, C_DG = 7168, 8192, 9216, 10240
C_MG = 11264
N_MAIN = 14336

ATT_TILE = 256
FOX_PAIRS = 2
DIFF_GROUP = 2
SSD_CHUNK = 128
SSD_SUBCHUNKS = 8
PROJ_TM, PROJ_TN = 2048, 1024
MERGE_TM = 512
VMEM_LIMIT = 56 * 1024 * 1024


def _sigmoid(x):
    return 0.5 * jnp.tanh(0.5 * x) + 0.5


def _silu(x):
    h = 0.5 * x
    return h + h * jnp.tanh(h)


def _softplus(x):
    return jnp.maximum(x, 0.0) + jnp.log1p(jnp.exp(-jnp.abs(x)))


def _split3(x):
    hi = x.astype(BF16)
    r1 = x - hi.astype(F32)
    mid = r1.astype(BF16)
    lo = (r1 - mid.astype(F32)).astype(BF16)
    return hi, mid, lo


def _dot(a, b):
    return jnp.dot(a, b, preferred_element_type=F32)


def _dot_nt(a, b):
    return lax.dot_general(a, b, (((1,), (1,)), ((), ())), preferred_element_type=F32)


def _exact_dot_l(m01, x):
    hi, mid, lo = _split3(x)
    return _dot(m01, hi) + _dot(m01, mid) + _dot(m01, lo)


def _tril_bf16(n):
    r = lax.broadcasted_iota(jnp.int32, (n, n), 0)
    c = lax.broadcasted_iota(jnp.int32, (n, n), 1)
    return jnp.where(r >= c, 1.0, 0.0).astype(BF16)


def _inproj_kernel(x_ref, nw_ref, w_ref, ws_ref, p_ref, s_ref, h_ref):
    @pl.when(pl.program_id(1) == 0)
    def _():
        x = x_ref[...]
        ms = jnp.mean(x * x, axis=-1, keepdims=True)
        h = (x * lax.rsqrt(ms + EPS) * nw_ref[...]).astype(BF16)
        h_ref[...] = h
        s_ref[...] = _dot_nt(h, ws_ref[...])

    p_ref[...] = _dot_nt(h_ref[...], w_ref[...]).astype(BF16)


def _inproj(x2, norm_w, w_main_t, w_small_t):
    t = x2.shape[0]
    tm, tn = min(PROJ_TM, t), PROJ_TN
    return pl.pallas_call(
        _inproj_kernel,
        out_shape=(jax.ShapeDtypeStruct((t, N_MAIN), BF16),
                   jax.ShapeDtypeStruct((t, LANES), F32)),
        grid=(t // tm, N_MAIN // tn),
        in_specs=[
            pl.BlockSpec((tm, D_MODEL), lambda i, j: (i, 0)),
            pl.BlockSpec((1, D_MODEL), lambda i, j: (0, 0)),
            pl.BlockSpec((tn, D_MODEL), lambda i, j: (j, 0)),
            pl.BlockSpec((LANES, D_MODEL), lambda i, j: (0, 0)),
        ],
        out_specs=(pl.BlockSpec((tm, tn), lambda i, j: (i, j)),
                   pl.BlockSpec((tm, LANES), lambda i, j: (i, 0))),
        scratch_shapes=[pltpu.VMEM((tm, D_MODEL), BF16)],
        compiler_params=pltpu.CompilerParams(
            dimension_semantics=("arbitrary", "arbitrary"), vmem_limit_bytes=VMEM_LIMIT),
        name="inproj",
    )(x2, norm_w, w_main_t, w_small_t)


def _fprep_kernel(s_ref, bias_ref, ft_ref, f_scr, *, seq):
    tril = _tril_bf16(LANES)
    carry = jnp.zeros((1, LANES), F32)
    for blk in range(seq // LANES):
        x = s_ref[blk * LANES:(blk + 1) * LANES, :] + bias_ref[...]
        log_f = jnp.minimum(x, 0.0) - jnp.log1p(jnp.exp(-jnp.abs(x)))
        c = _exact_dot_l(tril, log_f) + carry
        f_scr[blk * LANES:(blk + 1) * LANES, :] = c
        carry = c[LANES - 1:LANES, :]
    ft_ref[...] = f_scr[...].T[0:N_SMALL, :]


def _fprep(small, bias_pad, bsz, seq):
    return pl.pallas_call(
        functools.partial(_fprep_kernel, seq=seq),
        out_shape=jax.ShapeDtypeStruct((bsz, N_SMALL, seq), F32),
        grid=(bsz,),
        in_specs=[pl.BlockSpec((seq, LANES), lambda b: (b, 0)),
                  pl.BlockSpec((1, LANES), lambda b: (0, 0))],
        out_specs=pl.BlockSpec((None, N_SMALL, seq), lambda b: (b, 0, 0)),
        scratch_shapes=[pltpu.VMEM((seq, LANES), F32)],
        compiler_params=pltpu.CompilerParams(
            dimension_semantics=("arbitrary",), vmem_limit_bytes=VMEM_LIMIT),
        name="fprep",
    )(small, bias_pad)


def _sublane_tiles(x, op):
    t = x[0:8, :]
    for r in range(1, x.shape[0] // 8):
        t = op(t, x[8 * r:8 * (r + 1), :])
    return t


def _scores(k_refs, qts, h, j, tile, mask):
    s = _dot(k_refs[h][j * tile:(j + 1) * tile, :], qts[h])
    return s if mask is None else jnp.where(mask, s, -jnp.inf)


def _two_pass_sweep(n_tiles, prepare, make_queries, k_refs, vt_refs, scr, tile, mask, finish,
                    early_exp):
    n = len(k_refs)
    bufs = lambda i: [scr[n * (i % 2) + h] for h in range(n)]

    def pass1(i):
        prepare(i)
        qts, ms = make_queries(i), [[] for _ in range(n)]
        for h in range(n):
            for j in range(i + 1):
                s = _scores(k_refs, qts, h, j, tile, mask if j == i else None)
                t = _sublane_tiles(s, jnp.maximum)
                if early_exp:
                    m = jnp.max(t, axis=0, keepdims=True)
                    m = jnp.maximum(ms[h][-1], m) if ms[h] else m
                    ms[h].append(m)
                    bufs(i)[h][j * tile:(j + 1) * tile, :] = jnp.exp2(s - m).astype(BF16)
                else:
                    ms[h] = [jnp.maximum(ms[h][0], t) if ms[h] else t]
                    bufs(i)[h][j * tile:(j + 1) * tile, :] = s
        if not early_exp:
            ms = [[jnp.max(t[0], axis=0, keepdims=True)] for t in ms]
        return ms

    def pass2(i, ms):
        accs = [None] * n
        for h in range(n):
            for j in range(i + 1):
                w = bufs(i)[h][j * tile:(j + 1) * tile, :]
                if not early_exp:
                    w = jnp.exp2(w - ms[h][0]).astype(BF16)
                pv = _dot(vt_refs[h][:, j * tile:(j + 1) * tile], w)
                if early_exp and j < i:
                    pv = pv * jnp.exp2(ms[h][j] - ms[h][i])
                accs[h] = pv if accs[h] is None else accs[h] + pv
        return accs

    ms = pass1(0)
    for i in range(n_tiles):
        ms_next = pass1(i + 1) if i + 1 < n_tiles else None
        finish(i, pass2(i, ms))
        ms = ms_next


def _fox_kernel(q_ref, k_ref, v_ref, g_ref, ft_ref, o_ref, k_scr, vt_scr, *s_scr, seq, tile):
    lane = lax.broadcasted_iota(jnp.int32, (1, LANES), 1)
    sub = lax.broadcasted_iota(jnp.int32, (LANES, 1), 0)
    key = lax.broadcasted_iota(jnp.int32, (tile, tile), 0)
    qry = lax.broadcasted_iota(jnp.int32, (tile, tile), 1)
    causal = key <= qry
    own = (lane < HEAD_DIM, lane >= HEAD_DIM)
    own_t = (sub < HEAD_DIM, sub >= HEAD_DIM)
    bias_lo = (HEAD_DIM, 0)
    ones_q = tuple(jnp.logical_and(lane >= b, lane < b + 3).astype(F32) for b in bias_lo)
    streams = [(pr, hh) for pr in range(FOX_PAIRS) for hh in range(2)]

    def prepare(i):
        r0 = i * tile
        for pr in range(FOX_PAIRS):
            k = k_ref[r0:r0 + tile, pr * LANES:(pr + 1) * LANES]
            vt = v_ref[r0:r0 + tile, pr * LANES:(pr + 1) * LANES].astype(F32).T
            for hh in range(2):
                b = bias_lo[hh]
                hi, mid, lo = _split3(ft_ref[pr, hh:hh + 1, r0:r0 + tile] * (-LOG2E))
                rows = jnp.concatenate([hi.astype(F32), mid.astype(F32), lo.astype(F32),
                                        jnp.zeros((5, tile), F32)], axis=0)
                block = jnp.concatenate(([jnp.zeros((b, tile), F32)] if b else []) + [rows]
                                        + [jnp.zeros((LANES - b - 8, tile), F32)], axis=0)
                st = 2 * pr + hh
                k_scr[st, r0:r0 + tile, :] = jnp.where(own[hh], k, block.T.astype(BF16))
                vt_scr[st, :, r0:r0 + tile] = jnp.where(own_t[hh], vt, 1.0).astype(BF16)

    def queries(i):
        out = []
        for pr, hh in streams:
            q = q_ref[i * tile:(i + 1) * tile, pr * LANES:(pr + 1) * LANES].astype(F32)
            out.append(jnp.where(own[hh], q, ones_q[hh]).T.astype(BF16))
        return out

    def finish(i, accs):
        r0 = i * tile
        for pr in range(FOX_PAIRS):
            a0, a1 = accs[2 * pr], accs[2 * pr + 1]
            num = jnp.where(own_t[0], a0, a1)
            den = jnp.where(own_t[0], a0[HEAD_DIM:HEAD_DIM + 1, :], a1[0:1, :])
            g = g_ref[r0:r0 + tile, pr * LANES:(pr + 1) * LANES].astype(F32)
            o_ref[r0:r0 + tile, pr * LANES:(pr + 1) * LANES] = (
                (num / den).T * _silu(g)).astype(BF16)

    n = len(streams)
    _two_pass_sweep(seq // tile, prepare, queries, [k_scr.at[st] for st in range(n)],
                    [vt_scr.at[st] for st in range(n)], s_scr, tile, causal, finish, True)


def _fox(p, ft4, bsz, seq):
    tile = min(ATT_TILE, seq)
    w = FOX_PAIRS * LANES
    blk = lambda off: pl.BlockSpec((seq, w), lambda b, h, off=off: (b, off // w + h))
    n = 2 * FOX_PAIRS
    return pl.pallas_call(
        functools.partial(_fox_kernel, seq=seq, tile=tile),
        out_shape=jax.ShapeDtypeStruct((bsz * seq, D_MODEL), BF16),
        grid=(bsz, D_MODEL // w),
        in_specs=[blk(C_FQ), blk(C_FK), blk(C_FV), blk(C_FG),
                  pl.BlockSpec((None, FOX_PAIRS, 2, seq), lambda b, h: (b, h, 0, 0))],
        out_specs=pl.BlockSpec((seq, w), lambda b, h: (b, h)),
        scratch_shapes=[pltpu.VMEM((n, seq, LANES), BF16), pltpu.VMEM((n, LANES, seq), BF16)]
        + [pltpu.VMEM((seq, tile), BF16)] * (2 * n),
        compiler_params=pltpu.CompilerParams(
            dimension_semantics=("arbitrary", "arbitrary"), vmem_limit_bytes=VMEM_LIMIT),
        name="fox",
    )(p, p, p, p, ft4)


def _diff_kernel(q_ref, k_ref, v_ref, g_ref, cos_ref, s1_ref, s2_ref, lam_ref, sub_ref,
                 o_ref, k_scr, vt_scr, *s_scr, seq, tile, lam_init):
    def rope(x, r0):
        return (x * cos_ref[r0:r0 + tile, :]
                + pltpu.roll(x, LANES - ROT_HALF, 1) * s1_ref[r0:r0 + tile, :]
                + pltpu.roll(x, ROT_HALF, 1) * s2_ref[r0:r0 + tile, :])

    def prepare(i):
        r0 = i * tile
        for hd in range(DIFF_GROUP):
            cols = slice(hd * LANES, (hd + 1) * LANES)
            k_scr[hd, r0:r0 + tile, :] = rope(k_ref[r0:r0 + tile, cols].astype(F32), r0).astype(BF16)
            vt = v_ref[r0:r0 + tile, cols].astype(F32).T
            vt_scr[hd, :, r0:r0 + tile] = jnp.concatenate(
                [vt, jnp.ones((SUM_ROWS, tile), F32)], axis=0).astype(BF16)

    lp = lam_ref[...]
    lam = (jnp.exp(jnp.sum(lp[0:1] * lp[1:2], axis=1, keepdims=True))
           - jnp.exp(jnp.sum(lp[2:3] * lp[3:4], axis=1, keepdims=True)) + lam_init)

    lane = lax.broadcasted_iota(jnp.int32, (1, LANES), 1)
    first = lane < HEAD_DIM
    key = lax.broadcasted_iota(jnp.int32, (tile, tile), 0)
    qry = lax.broadcasted_iota(jnp.int32, (tile, tile), 1)
    chunk_causal = (key // MASK_CHUNK) <= (qry // MASK_CHUNK)

    def queries(i):
        r0 = i * tile
        out = []
        for hd in range(DIFF_GROUP):
            q = rope(q_ref[r0:r0 + tile, hd * LANES:(hd + 1) * LANES].astype(F32), r0)
            out += [jnp.where(first, q, 0.0).T.astype(BF16), jnp.where(first, 0.0, q).T.astype(BF16)]
        return out

    def finish(i, accs):
        r0 = i * tile
        for hd in range(DIFF_GROUP):
            a0, a1 = accs[2 * hd], accs[2 * hd + 1]
            o = (a0[0:LANES] / a0[LANES:LANES + 1] - lam * (a1[0:LANES] / a1[LANES:LANES + 1])).T
            ms = jnp.mean(o * o, axis=-1, keepdims=True)
            o = o * lax.rsqrt(ms + EPS) * sub_ref[...] * (1.0 - lam_init)
            g = g_ref[r0:r0 + tile, hd * LANES:(hd + 1) * LANES].astype(F32)
            o_ref[r0:r0 + tile, hd * LANES:(hd + 1) * LANES] = (o * _silu(g)).astype(BF16)

    heads = [hd for hd in range(DIFF_GROUP) for _ in range(2)]
    _two_pass_sweep(seq // tile, prepare, queries, [k_scr.at[hd] for hd in heads],
                    [vt_scr.at[hd] for hd in heads], s_scr, tile, chunk_causal, finish, False)


def _diff(p, cos_t, s1_t, s2_t, lam_p, sub_w, bsz, seq, lam_init):
    tile = min(ATT_TILE, seq)
    w = DIFF_GROUP * LANES
    blk = lambda off: pl.BlockSpec((seq, w), lambda b, h, off=off: (b, off // w + h))
    const = lambda shape: pl.BlockSpec(shape, lambda b, h: (0, 0))
    return pl.pallas_call(
        functools.partial(_diff_kernel, seq=seq, tile=tile, lam_init=lam_init),
        out_shape=jax.ShapeDtypeStruct((bsz * seq, D_MODEL), BF16),
        grid=(bsz, D_MODEL // w),
        in_specs=[blk(C_DQ), blk(C_DK), blk(C_DV), blk(C_DG),
                  const((seq, LANES)), const((seq, LANES)), const((seq, LANES)),
                  const((4, HEAD_DIM)), const((1, LANES))],
        out_specs=pl.BlockSpec((seq, w), lambda b, h: (b, h)),
        scratch_shapes=[pltpu.VMEM((DIFF_GROUP, seq, LANES), BF16),
                        pltpu.VMEM((DIFF_GROUP, LANES + SUM_ROWS, seq), BF16)]
        + [pltpu.VMEM((seq, tile), F32)] * (4 * DIFF_GROUP),
        compiler_params=pltpu.CompilerParams(
            dimension_semantics=("arbitrary", "arbitrary"), vmem_limit_bytes=VMEM_LIMIT),
        name="diffattn",
    )(p, p, p, p, cos_t, s1_t, s2_t, lam_p, sub_w)


def _ssd_kernel(xbc_ref, z_ref, sm_ref, cw_ref, cb_ref, dtb_ref, alog_ref, dskip_ref, nw_ref,
                e_ref, shift_ref, tril_ref, o_ref, u_scr, st_scr, *, q, nsub):
    @pl.when(pl.program_id(1) == 0)
    def _():
        u_scr[0:CONV_PAD, :] = jnp.zeros((CONV_PAD, 2 * D_MODEL), BF16)
        st_scr[...] = jnp.zeros_like(st_scr)

    rows = q * nsub
    u_scr[CONV_PAD:CONV_PAD + rows, :] = xbc_ref[...]
    lane = lax.broadcasted_iota(jnp.int32, (1, LANES), 1)
    is_head = jnp.logical_and(lane >= N_SMALL, lane < 2 * N_SMALL)
    a_neg = jnp.where(is_head, -jnp.exp(alog_ref[...]), 0.0)
    row = lax.broadcasted_iota(jnp.int32, (q, q), 0)
    col = lax.broadcasted_iota(jnp.int32, (q, q), 1)
    tril = col <= row
    first = lane < HEAD_DIM

    for sc in range(nsub):
        r0 = sc * q
        delayed = _dot(shift_ref[...], u_scr[r0:r0 + CONV_PAD + q, :])
        conv = cb_ref[...] + cw_ref[CONV_K - 1:CONV_K, :] * xbc_ref[r0:r0 + q, :].astype(F32)
        for kk in range(CONV_K - 1):
            conv = conv + cw_ref[kk:kk + 1, :] * delayed[kk * q:(kk + 1) * q, :]
        act = _silu(conv)
        xs = act[:, 0:D_MODEL]

        dt = _softplus(sm_ref[r0:r0 + q, :] + dtb_ref[...])
        a_cs = _dot(tril_ref[...], jnp.concatenate(_split3(dt * a_neg), axis=0))
        ea = jnp.exp(a_cs)
        eds = jnp.exp(a_cs[q - 1:q, :] - a_cs)
        stacked = jnp.concatenate([ea, eds * dt], axis=0)
        hi = stacked.astype(BF16)
        mid = (stacked - hi.astype(F32)).astype(BF16)
        wide = _dot(jnp.concatenate([hi, mid], axis=1), e_ref[...])
        ea_x, w_x = wide[0:q], wide[q:2 * q]
        a_cs_t = a_cs.T
        dt_t = dt.T

        xs_b = xs.astype(BF16)
        xdt_s = (xs * w_x).astype(BF16)

        ys = []
        for g in range(SSM_GROUPS):
            lo = g * SSM_GROUP_W
            bg = act[:, D_MODEL + g * SSM_STATE:D_MODEL + (g + 1) * SSM_STATE].astype(BF16)
            cg = act[:, D_MODEL + SSM_GROUPS * SSM_STATE + g * SSM_STATE:
                     D_MODEL + SSM_GROUPS * SSM_STATE + (g + 1) * SSM_STATE].astype(BF16)
            cb = _dot_nt(cg, bg)
            state = st_scr[g]
            y_off = _dot(cg, state.astype(BF16)) * ea_x[:, lo:lo + SSM_GROUP_W]
            y_diag = []
            for pp in range(2):
                xp = xs_b[:, lo + pp * LANES:lo + (pp + 1) * LANES]
                zero = jnp.zeros_like(xp)
                x_blocks = jnp.concatenate(
                    [jnp.where(first, xp, zero), jnp.where(first, zero, xp)], axis=0)
                mats = []
                for hh in range(2):
                    hl = N_SMALL + g * 4 + pp * 2 + hh
                    seg = a_cs[:, hl:hl + 1] - a_cs_t[hl:hl + 1, :]
                    decay = jnp.where(tril, jnp.exp(seg), 0.0) * dt_t[hl:hl + 1, :]
                    mats.append((cb * decay).astype(BF16))
                y_diag.append(_dot(jnp.concatenate(mats, axis=1), x_blocks))
            new_states = lax.dot_general(bg, xdt_s[:, lo:lo + SSM_GROUP_W],
                                         (((0,), (0,)), ((), ())), preferred_element_type=F32)
            st_scr[g] = state * ea_x[q - 1:q, lo:lo + SSM_GROUP_W] + new_states
            ys.append(jnp.concatenate(y_diag, axis=1) + y_off)

        y = jnp.concatenate(ys, axis=1) + xs * dskip_ref[...]
        z = z_ref[r0:r0 + q, :].astype(F32)
        yg = y * _silu(z)
        outs = []
        for g in range(SSM_GROUPS):
            v = yg[:, g * SSM_GROUP_W:(g + 1) * SSM_GROUP_W]
            ms = jnp.mean(v * v, axis=-1, keepdims=True)
            outs.append(v * lax.rsqrt(ms + EPS))
        o_ref[r0:r0 + q, :] = (jnp.concatenate(outs, axis=1) * nw_ref[...]).astype(BF16)

    u_scr[CONV_PAD - CONV_HALO:CONV_PAD, :] = xbc_ref[rows - CONV_HALO:rows, :]


def _conv_shift_matrix(q):
    m = np.zeros((3 * q, CONV_PAD + q), np.float32)
    for k in range(CONV_K - 1):
        t = np.arange(q)
        m[k * q + t, CONV_PAD + t - (CONV_K - 1) + k] = 1.0
    return jnp.asarray(m, BF16)


def _ssd(p, small, conv_w, conv_b, dtb_pad, alog_pad, dskip_x, norm_w, expand, bsz, seq):
    q = min(SSD_CHUNK, seq)
    nsub = SSD_SUBCHUNKS if seq % (q * SSD_SUBCHUNKS) == 0 else 1
    rows = q * nsub
    nc = seq // rows
    const = lambda shape: pl.BlockSpec(shape, lambda b, c: (0, 0))
    return pl.pallas_call(
        functools.partial(_ssd_kernel, q=q, nsub=nsub),
        out_shape=jax.ShapeDtypeStruct((bsz * seq, D_MODEL), BF16),
        grid=(bsz, nc),
        in_specs=[
            pl.BlockSpec((rows, 2 * D_MODEL), lambda b, c: (b * nc + c, C_XBC // (2 * D_MODEL))),
            pl.BlockSpec((rows, D_MODEL), lambda b, c: (b * nc + c, C_SZ // D_MODEL)),
            pl.BlockSpec((rows, LANES), lambda b, c: (b * nc + c, 0)),
            const((CONV_K, 2 * D_MODEL)), const((1, 2 * D_MODEL)),
            const((1, LANES)), const((1, LANES)),
            const((1, D_MODEL)), const((1, D_MODEL)),
            const((2 * LANES, D_MODEL)), const((3 * q, CONV_PAD + q)), const((q, 3 * q)),
        ],
        out_specs=pl.BlockSpec((rows, D_MODEL), lambda b, c: (b * nc + c, 0)),
        scratch_shapes=[pltpu.VMEM((CONV_PAD + rows, 2 * D_MODEL), BF16),
                        pltpu.VMEM((SSM_GROUPS, SSM_STATE, SSM_GROUP_W), F32)],
        compiler_params=pltpu.CompilerParams(
            dimension_semantics=("arbitrary", "arbitrary"), vmem_limit_bytes=VMEM_LIMIT),
        name="ssd",
    )(p, p, small, conv_w, conv_b, dtb_pad, alog_pad, dskip_x, norm_w, expand,
      _conv_shift_matrix(q), jnp.asarray(np.tile(np.tril(np.ones((q, q), np.float32)), (1, 3)), BF16))


def _merge_kernel(x_ref, ya_ref, yb_ref, yc_ref, g0_ref, g1_ref, g2_ref, wb_ref, wo_ref,
                  fw_ref, o_ref, *, final):
    merged = None
    for n, (y_ref, g_ref) in enumerate(((ya_ref, g0_ref), (yb_ref, g1_ref), (yc_ref, g2_ref))):
        term = _sigmoid(g_ref[...].astype(F32)) * _dot(y_ref[...], wb_ref[n])
        merged = term if merged is None else merged + term
    out = x_ref[...] + _dot(merged.astype(BF16), wo_ref[...])
    if final:
        ms = jnp.mean(out * out, axis=-1, keepdims=True)
        out = out * lax.rsqrt(ms + EPS) * fw_ref[...]
    o_ref[...] = out


def _merge(x2, y_a, y_b, y_c, p, w_branch, w_out, final_w, final):
    t = x2.shape[0]
    tm = min(MERGE_TM, t)
    rows = lambda: pl.BlockSpec((tm, D_MODEL), lambda i: (i, 0))
    gate = lambda n: pl.BlockSpec((tm, D_MODEL), lambda i, n=n: (i, C_MG // D_MODEL + n))
    return pl.pallas_call(
        functools.partial(_merge_kernel, final=final),
        out_shape=jax.ShapeDtypeStruct((t, D_MODEL), F32),
        grid=(t // tm,),
        in_specs=[rows(), rows(), rows(), rows(), gate(0), gate(1), gate(2),
                  pl.BlockSpec((3, D_MODEL, D_MODEL), lambda i: (0, 0, 0)),
                  pl.BlockSpec((D_MODEL, D_MODEL), lambda i: (0, 0)),
                  pl.BlockSpec((1, D_MODEL), lambda i: (0, 0))],
        out_specs=rows(),
        compiler_params=pltpu.CompilerParams(
            dimension_semantics=("arbitrary",), vmem_limit_bytes=VMEM_LIMIT),
        name="merge",
    )(x2, y_a, y_b, y_c, p, p, p, w_branch, w_out, final_w)


def _rope_tables(seq):
    pos = jnp.arange(seq, dtype=F32)
    inv_freq = ROPE_THETA ** (-jnp.arange(0, 2 * ROT_HALF, 2, dtype=F32) / (2 * ROT_HALF))
    ang = pos[:, None] * inv_freq[None, :]
    cos, sin = jnp.cos(ang), jnp.sin(ang)
    ones = jnp.ones((seq, HEAD_DIM - 2 * ROT_HALF), F32)
    zeros = jnp.zeros((seq, ROT_HALF), F32)
    rest = jnp.zeros((seq, HEAD_DIM - 2 * ROT_HALF), F32)
    cos_c = jnp.concatenate([cos, cos, ones], axis=1)
    s1_c = jnp.concatenate([-sin, zeros, rest], axis=1)
    s2_c = jnp.concatenate([zeros, sin, rest], axis=1)
    two = lambda t: jnp.concatenate([t, t], axis=1)
    return two(cos_c), two(s1_c), two(s2_c)


def _pad_lanes(v, start):
    return jnp.zeros((1, LANES), F32).at[0, start:start + v.shape[0]].set(v.astype(F32))


_SEGMENTS = ((0, C_FQ, 1024, True), (1024, C_FK, 1024, False), (2048, C_FV, 1024, False),
             (3088, C_FG, 1024, False), (5136, C_XBC, 2048, False), (4112, C_SZ, 1024, False),
             (7200, C_DQ, 1024, True), (8224, C_DK, 1024, False), (9248, C_DV, 1024, False),
             (10272, C_DG, 1024, False), (11296, C_MG, 3072, False))
_SRC_FORGET, _SRC_DT, _N_IN = 3072, 7184, 14368


def _wprep_kernel(wt_ref, wm_ref, ws_ref):
    scale = LOG2E * HEAD_DIM ** -0.5
    for src, dst, n, scaled in _SEGMENTS:
        x = wt_ref[src:src + n, :]
        wm_ref[dst:dst + n, :] = (x * scale if scaled else x).astype(BF16)
    ws_ref[...] = jnp.concatenate(
        [wt_ref[_SRC_FORGET:_SRC_FORGET + N_SMALL, :], wt_ref[_SRC_DT:_SRC_DT + N_SMALL, :],
         jnp.zeros((LANES - 2 * N_SMALL, wt_ref.shape[1]), F32)], axis=0).astype(BF16)


def _projection_weights(w_in_t, layer):
    return pl.pallas_call(
        _wprep_kernel,
        out_shape=(jax.ShapeDtypeStruct((N_MAIN, D_MODEL), BF16),
                   jax.ShapeDtypeStruct((LANES, D_MODEL), BF16)),
        grid=(D_MODEL // LANES,),
        in_specs=[pl.BlockSpec((None, _N_IN, LANES), lambda i: (layer, 0, i))],
        out_specs=(pl.BlockSpec((N_MAIN, LANES), lambda i: (0, i)),
                   pl.BlockSpec((LANES, LANES), lambda i: (0, i))),
        compiler_params=pltpu.CompilerParams(
            dimension_semantics=("arbitrary",), vmem_limit_bytes=VMEM_LIMIT),
        name="wprep",
    )(w_in_t)


def kernel(x, norm_w, w_in, b_forget, conv_w, conv_b, dt_bias, a_log, d_skip, ssm_norm_w,
           diff_lambda, subln_w, w_branch, w_out, final_norm_w):
    bsz, seq, _ = x.shape
    depth = norm_w.shape[0]
    cos_t, s1_t, s2_t = _rope_tables(seq)
    heads = jnp.arange(D_MODEL) // HEAD_DIM
    expand = (jnp.arange(2 * LANES)[:, None] % LANES == (heads[None, :] + N_SMALL)).astype(BF16)

    w_in_t = jnp.swapaxes(w_in, 1, 2)
    w_branch_b, w_out_b = w_branch.astype(BF16), w_out.astype(BF16)

    x2 = x.reshape(bsz * seq, D_MODEL)
    for layer in range(depth):
        w_main_t, w_small_t = _projection_weights(w_in_t, layer)
        p, small = _inproj(x2, norm_w[layer][None, :], w_main_t, w_small_t)

        ft = _fprep(small, _pad_lanes(b_forget[layer], 0), bsz, seq)
        y_a = _fox(p, ft.reshape(bsz, N_SMALL // 2, 2, seq), bsz, seq)

        y_b = _ssd(p, small, conv_w[layer], conv_b[layer][None, :],
                   _pad_lanes(dt_bias[layer], N_SMALL),
                   _pad_lanes(a_log[layer], N_SMALL),
                   jnp.repeat(d_skip[layer], HEAD_DIM)[None, :],
                   ssm_norm_w[layer][None, :], expand, bsz, seq)

        lam_init = 0.8 - 0.6 * math.exp(-0.3 * layer)
        y_c = _diff(p, cos_t, s1_t, s2_t, diff_lambda[layer], subln_w[layer][None, :],
                    bsz, seq, lam_init)

        x2 = _merge(x2, y_a, y_b, y_c, p, w_branch_b[layer], w_out_b[layer],
                    final_norm_w[None, :], layer == depth - 1)
    return x2.reshape(bsz, seq, D_MODEL)
```

```python
import functools
import math

import numpy as np
import jax
import jax.numpy as jnp
from jax import lax
from jax.experimental import pallas as pl
from jax.experimental.pallas import tpu as pltpu

F32 = jnp.float32
BF16 = jnp.bfloat16

D_MODEL = 1024
EPS = 1e-6
LOG2E = 1.4426950408889634
HEAD_DIM = 64
LANES = 128
N_SMALL = 16
SSM_GROUPS = 4
SSM_STATE = 128
SSM_GROUP_W = 256
CONV_K = 4
CONV_HALO = 16
CONV_PAD = 128
ROT_HALF = 8
ROPE_THETA = 500000.0
MASK_CHUNK = 64
SUM_ROWS = 16

C_FQ, C_FK, C_FV, C_FG = 0, 1024, 2048, 3072
C_XBC, C_SZ = 4096, 6144
C_DQ, C_DK, C_DV, C_DG = 7168, 8192, 9216, 10240
C_MG = 11264
N_MAIN = 14336

ATT_TILE = 256
FOX_PAIRS = 2
DIFF_GROUP = 2
SSD_CHUNK = 128
SSD_SUBCHUNKS = 4
PROJ_TM, PROJ_TN = 2048, 1024
MERGE_TM = 512
VMEM_LIMIT = 56 * 1024 * 1024


def _sigmoid(x):
    return 0.5 * jnp.tanh(0.5 * x) + 0.5


def _silu(x):
    h = 0.5 * x
    return h + h * jnp.tanh(h)


def _softplus(x):
    return jnp.maximum(x, 0.0) + jnp.log1p(jnp.exp(-jnp.abs(x)))


def _split3(x):
    hi = x.astype(BF16)
    r1 = x - hi.astype(F32)
    mid = r1.astype(BF16)
    lo = (r1 - mid.astype(F32)).astype(BF16)
    return hi, mid, lo


def _dot(a, b):
    return jnp.dot(a, b, preferred_element_type=F32)


def _dot_nt(a, b):
    return lax.dot_general(a, b, (((1,), (1,)), ((), ())), preferred_element_type=F32)


def _exact_dot_l(m01, x):
    hi, mid, lo = _split3(x)
    return _dot(m01, hi) + _dot(m01, mid) + _dot(m01, lo)


def _tril_bf16(n):
    r = lax.broadcasted_iota(jnp.int32, (n, n), 0)
    c = lax.broadcasted_iota(jnp.int32, (n, n), 1)
    return jnp.where(r >= c, 1.0, 0.0).astype(BF16)


def _inproj_kernel(x_ref, nw_ref, w_ref, ws_ref, p_ref, s_ref, h_ref):
    @pl.when(pl.program_id(1) == 0)
    def _():
        x = x_ref[...]
        ms = jnp.mean(x * x, axis=-1, keepdims=True)
        h = (x * lax.rsqrt(ms + EPS) * nw_ref[...]).astype(BF16)
        h_ref[...] = h
        s_ref[...] = _dot_nt(h, ws_ref[...])

    p_ref[...] = _dot_nt(h_ref[...], w_ref[...]).astype(BF16)


def _inproj(x2, norm_w, w_main_t, w_small_t):
    t = x2.shape[0]
    tm, tn = min(PROJ_TM, t), PROJ_TN
    return pl.pallas_call(
        _inproj_kernel,
        out_shape=(jax.ShapeDtypeStruct((t, N_MAIN), BF16),
                   jax.ShapeDtypeStruct((t, LANES), F32)),
        grid=(t // tm, N_MAIN // tn),
        in_specs=[
            pl.BlockSpec((tm, D_MODEL), lambda i, j: (i, 0)),
            pl.BlockSpec((1, D_MODEL), lambda i, j: (0, 0)),
            pl.BlockSpec((tn, D_MODEL), lambda i, j: (j, 0)),
            pl.BlockSpec((LANES, D_MODEL), lambda i, j: (0, 0)),
        ],
        out_specs=(pl.BlockSpec((tm, tn), lambda i, j: (i, j)),
                   pl.BlockSpec((tm, LANES), lambda i, j: (i, 0))),
        scratch_shapes=[pltpu.VMEM((tm, D_MODEL), BF16)],
        compiler_params=pltpu.CompilerParams(
            dimension_semantics=("arbitrary", "arbitrary"), vmem_limit_bytes=VMEM_LIMIT),
        name="inproj",
    )(x2, norm_w, w_main_t, w_small_t)


def _fprep_kernel(s_ref, bias_ref, ft_ref, f_scr, *, seq):
    tril = _tril_bf16(LANES)
    carry = jnp.zeros((1, LANES), F32)
    for blk in range(seq // LANES):
        x = s_ref[blk * LANES:(blk + 1) * LANES, :] + bias_ref[...]
        log_f = jnp.minimum(x, 0.0) - jnp.log1p(jnp.exp(-jnp.abs(x)))
        c = _exact_dot_l(tril, log_f) + carry
        f_scr[blk * LANES:(blk + 1) * LANES, :] = c
        carry = c[LANES - 1:LANES, :]
    ft_ref[...] = f_scr[...].T[0:N_SMALL, :]


def _fprep(small, bias_pad, bsz, seq):
    return pl.pallas_call(
        functools.partial(_fprep_kernel, seq=seq),
        out_shape=jax.ShapeDtypeStruct((bsz, N_SMALL, seq), F32),
        grid=(bsz,),
        in_specs=[pl.BlockSpec((seq, LANES), lambda b: (b, 0)),
                  pl.BlockSpec((1, LANES), lambda b: (0, 0))],
        out_specs=pl.BlockSpec((None, N_SMALL, seq), lambda b: (b, 0, 0)),
        scratch_shapes=[pltpu.VMEM((seq, LANES), F32)],
        compiler_params=pltpu.CompilerParams(
            dimension_semantics=("arbitrary",), vmem_limit_bytes=VMEM_LIMIT),
        name="fprep",
    )(small, bias_pad)


def _sublane_tiles(x, op):
    t = x[0:8, :]
    for r in range(1, x.shape[0] // 8):
        t = op(t, x[8 * r:8 * (r + 1), :])
    return t


def _scores(k_refs, qts, h, j, tile, mask):
    s = _dot(k_refs[h][j * tile:(j + 1) * tile, :], qts[h])
    return s if mask is None else jnp.where(mask, s, -jnp.inf)


def _two_pass_sweep(n_tiles, prepare, make_queries, k_refs, vt_refs, scr, tile, mask, finish,
                    early_exp):
    n = len(k_refs)
    bufs = lambda i: [scr[n * (i % 2) + h] for h in range(n)]

    def pass1(i):
        prepare(i)
        qts, ms = make_queries(i), [[] for _ in range(n)]
        for h in range(n):
            for j in range(i + 1):
                s = _scores(k_refs, qts, h, j, tile, mask if j == i else None)
                t = _sublane_tiles(s, jnp.maximum)
                if early_exp:
                    m = jnp.max(t, axis=0, keepdims=True)
                    m = jnp.maximum(ms[h][-1], m) if ms[h] else m
                    ms[h].append(m)
                    bufs(i)[h][j * tile:(j + 1) * tile, :] = jnp.exp2(s - m).astype(BF16)
                else:
                    ms[h] = [jnp.maximum(ms[h][0], t) if ms[h] else t]
                    bufs(i)[h][j * tile:(j + 1) * tile, :] = s
        if not early_exp:
            ms = [[jnp.max(t[0], axis=0, keepdims=True)] for t in ms]
        return ms

    def pass2(i, ms):
        accs = [None] * n
        for h in range(n):
            for j in range(i + 1):
                w = bufs(i)[h][j * tile:(j + 1) * tile, :]
                if not early_exp:
                    w = jnp.exp2(w - ms[h][0]).astype(BF16)
                pv = _dot(vt_refs[h][:, j * tile:(j + 1) * tile], w)
                if early_exp and j < i:
                    pv = pv * jnp.exp2(ms[h][j] - ms[h][i])
                accs[h] = pv if accs[h] is None else accs[h] + pv
        return accs

    ms = pass1(0)
    for i in range(n_tiles):
        ms_next = pass1(i + 1) if i + 1 < n_tiles else None
        finish(i, pass2(i, ms))
        ms = ms_next


def _fox_kernel(q_ref, k_ref, v_ref, g_ref, ft_ref, o_ref, k_scr, vt_scr, *s_scr, seq, tile):
    lane = lax.broadcasted_iota(jnp.int32, (1, LANES), 1)
    sub = lax.broadcasted_iota(jnp.int32, (LANES, 1), 0)
    key = lax.broadcasted_iota(jnp.int32, (tile, tile), 0)
    qry = lax.broadcasted_iota(jnp.int32, (tile, tile), 1)
    causal = key <= qry
    own = (lane < HEAD_DIM, lane >= HEAD_DIM)
    own_t = (sub < HEAD_DIM, sub >= HEAD_DIM)
    bias_lo = (HEAD_DIM, 0)
    ones_q = tuple(jnp.logical_and(lane >= b, lane < b + 3).astype(F32) for b in bias_lo)
    streams = [(pr, hh) for pr in range(FOX_PAIRS) for hh in range(2)]

    def prepare(i):
        r0 = i * tile
        for pr in range(FOX_PAIRS):
            k = k_ref[r0:r0 + tile, pr * LANES:(pr + 1) * LANES]
            vt = v_ref[r0:r0 + tile, pr * LANES:(pr + 1) * LANES].astype(F32).T
            for hh in range(2):
                b = bias_lo[hh]
                hi, mid, lo = _split3(ft_ref[pr, hh:hh + 1, r0:r0 + tile] * (-LOG2E))
                rows = jnp.concatenate([hi.astype(F32), mid.astype(F32), lo.astype(F32),
                                        jnp.zeros((5, tile), F32)], axis=0)
                block = jnp.concatenate(([jnp.zeros((b, tile), F32)] if b else []) + [rows]
                                        + [jnp.zeros((LANES - b - 8, tile), F32)], axis=0)
                st = 2 * pr + hh
                k_scr[st, r0:r0 + tile, :] = jnp.where(own[hh], k, block.T.astype(BF16))
                vt_scr[st, :, r0:r0 + tile] = jnp.where(own_t[hh], vt, 1.0).astype(BF16)

    def queries(i):
        out = []
        for pr, hh in streams:
            q = q_ref[i * tile:(i + 1) * tile, pr * LANES:(pr + 1) * LANES].astype(F32)
            out.append(jnp.where(own[hh], q, ones_q[hh]).T.astype(BF16))
        return out

    def finish(i, accs):
        r0 = i * tile
        for pr in range(FOX_PAIRS):
            a0, a1 = accs[2 * pr], accs[2 * pr + 1]
            num = jnp.where(own_t[0], a0, a1)
            den = jnp.where(own_t[0], a0[HEAD_DIM:HEAD_DIM + 1, :], a1[0:1, :])
            g = g_ref[r0:r0 + tile, pr * LANES:(pr + 1) * LANES].astype(F32)
            o_ref[r0:r0 + tile, pr * LANES:(pr + 1) * LANES] = (
                (num / den).T * _silu(g)).astype(BF16)

    n = len(streams)
    _two_pass_sweep(seq // tile, prepare, queries, [k_scr.at[st] for st in range(n)],
                    [vt_scr.at[st] for st in range(n)], s_scr, tile, causal, finish, True)


def _fox(p, ft4, bsz, seq):
    tile = min(ATT_TILE, seq)
    w = FOX_PAIRS * LANES
    blk = lambda off: pl.BlockSpec((seq, w), lambda b, h, off=off: (b, off // w + h))
    n = 2 * FOX_PAIRS
    return pl.pallas_call(
        functools.partial(_fox_kernel, seq=seq, tile=tile),
        out_shape=jax.ShapeDtypeStruct((bsz * seq, D_MODEL), BF16),
        grid=(bsz, D_MODEL // w),
        in_specs=[blk(C_FQ), blk(C_FK), blk(C_FV), blk(C_FG),
                  pl.BlockSpec((None, FOX_PAIRS, 2, seq), lambda b, h: (b, h, 0, 0))],
        out_specs=pl.BlockSpec((seq, w), lambda b, h: (b, h)),
        scratch_shapes=[pltpu.VMEM((n, seq, LANES), BF16), pltpu.VMEM((n, LANES, seq), BF16)]
        + [pltpu.VMEM((seq, tile), BF16)] * (2 * n),
        compiler_params=pltpu.CompilerParams(
            dimension_semantics=("arbitrary", "arbitrary"), vmem_limit_bytes=VMEM_LIMIT),
        name="fox",
    )(p, p, p, p, ft4)


def _diff_kernel(q_ref, k_ref, v_ref, g_ref, cos_ref, s1_ref, s2_ref, lam_ref, sub_ref,
                 o_ref, k_scr, vt_scr, *s_scr, seq, tile, lam_init):
    def rope(x, r0):
        return (x * cos_ref[r0:r0 + tile, :]
                + pltpu.roll(x, LANES - ROT_HALF, 1) * s1_ref[r0:r0 + tile, :]
                + pltpu.roll(x, ROT_HALF, 1) * s2_ref[r0:r0 + tile, :])

    def prepare(i):
        r0 = i * tile
        for hd in range(DIFF_GROUP):
            cols = slice(hd * LANES, (hd + 1) * LANES)
            k_scr[hd, r0:r0 + tile, :] = rope(k_ref[r0:r0 + tile, cols].astype(F32), r0).astype(BF16)
            vt = v_ref[r0:r0 + tile, cols].astype(F32).T
            vt_scr[hd, :, r0:r0 + tile] = jnp.concatenate(
                [vt, jnp.ones((SUM_ROWS, tile), F32)], axis=0).astype(BF16)

    lp = lam_ref[...]
    lam = (jnp.exp(jnp.sum(lp[0:1] * lp[1:2], axis=1, keepdims=True))
           - jnp.exp(jnp.sum(lp[2:3] * lp[3:4], axis=1, keepdims=True)) + lam_init)

    lane = lax.broadcasted_iota(jnp.int32, (1, LANES), 1)
    first = lane < HEAD_DIM
    key = lax.broadcasted_iota(jnp.int32, (tile, tile), 0)
    qry = lax.broadcasted_iota(jnp.int32, (tile, tile), 1)
    chunk_causal = (key // MASK_CHUNK) <= (qry // MASK_CHUNK)

    def queries(i):
        r0 = i * tile
        out = []
        for hd in range(DIFF_GROUP):
            q = rope(q_ref[r0:r0 + tile, hd * LANES:(hd + 1) * LANES].astype(F32), r0)
            out += [jnp.where(first, q, 0.0).T.astype(BF16), jnp.where(first, 0.0, q).T.astype(BF16)]
        return out

    def finish(i, accs):
        r0 = i * tile
        for hd in range(DIFF_GROUP):
            a0, a1 = accs[2 * hd], accs[2 * hd + 1]
            o = (a0[0:LANES] / a0[LANES:LANES + 1] - lam * (a1[0:LANES] / a1[LANES:LANES + 1])).T
            ms = jnp.mean(o * o, axis=-1, keepdims=True)
            o = o * lax.rsqrt(ms + EPS) * sub_ref[...] * (1.0 - lam_init)
            g = g_ref[r0:r0 + tile, hd * LANES:(hd + 1) * LANES].astype(F32)
            o_ref[r0:r0 + tile, hd * LANES:(hd + 1) * LANES] = (o * _silu(g)).astype(BF16)

    heads = [hd for hd in range(DIFF_GROUP) for _ in range(2)]
    _two_pass_sweep(seq // tile, prepare, queries, [k_scr.at[hd] for hd in heads],
                    [vt_scr.at[hd] for hd in heads], s_scr, tile, chunk_causal, finish, False)


def _diff(p, cos_t, s1_t, s2_t, lam_p, sub_w, bsz, seq, lam_init):
    tile = min(ATT_TILE, seq)
    w = DIFF_GROUP * LANES
    blk = lambda off: pl.BlockSpec((seq, w), lambda b, h, off=off: (b, off // w + h))
    const = lambda shape: pl.BlockSpec(shape, lambda b, h: (0, 0))
    return pl.pallas_call(
        functools.partial(_diff_kernel, seq=seq, tile=tile, lam_init=lam_init),
        out_shape=jax.ShapeDtypeStruct((bsz * seq, D_MODEL), BF16),
        grid=(bsz, D_MODEL // w),
        in_specs=[blk(C_DQ), blk(C_DK), blk(C_DV), blk(C_DG),
                  const((seq, LANES)), const((seq, LANES)), const((seq, LANES)),
                  const((4, HEAD_DIM)), const((1, LANES))],
        out_specs=pl.BlockSpec((seq, w), lambda b, h: (b, h)),
        scratch_shapes=[pltpu.VMEM((DIFF_GROUP, seq, LANES), BF16),
                        pltpu.VMEM((DIFF_GROUP, LANES + SUM_ROWS, seq), BF16)]
        + [pltpu.VMEM((seq, tile), F32)] * (4 * DIFF_GROUP),
        compiler_params=pltpu.CompilerParams(
            dimension_semantics=("arbitrary", "arbitrary"), vmem_limit_bytes=VMEM_LIMIT),
        name="diffattn",
    )(p, p, p, p, cos_t, s1_t, s2_t, lam_p, sub_w)


def _ssd_kernel(xbc_ref, z_ref, sm_ref, cw_ref, cb_ref, dtb_ref, alog_ref, dskip_ref, nw_ref,
                e_ref, shift_ref, tril_ref, o_ref, u_scr, st_scr, *, q, nsub):
    @pl.when(pl.program_id(1) == 0)
    def _():
        u_scr[0:CONV_PAD, :] = jnp.zeros((CONV_PAD, 2 * D_MODEL), BF16)
        st_scr[...] = jnp.zeros_like(st_scr)

    rows = q * nsub
    u_scr[CONV_PAD:CONV_PAD + rows, :] = xbc_ref[...]
    lane = lax.broadcasted_iota(jnp.int32, (1, LANES), 1)
    is_head = jnp.logical_and(lane >= N_SMALL, lane < 2 * N_SMALL)
    a_neg = jnp.where(is_head, -jnp.exp(alog_ref[...]), 0.0)
    row = lax.broadcasted_iota(jnp.int32, (q, q), 0)
    col = lax.broadcasted_iota(jnp.int32, (q, q), 1)
    tril = col <= row
    first = lane < HEAD_DIM

    for sc in range(nsub):
        r0 = sc * q
        delayed = _dot(shift_ref[...], u_scr[r0:r0 + CONV_PAD + q, :])
        conv = cb_ref[...] + cw_ref[CONV_K - 1:CONV_K, :] * xbc_ref[r0:r0 + q, :].astype(F32)
        for kk in range(CONV_K - 1):
            conv = conv + cw_ref[kk:kk + 1, :] * delayed[kk * q:(kk + 1) * q, :]
        act = _silu(conv)
        xs = act[:, 0:D_MODEL]

        dt = _softplus(sm_ref[r0:r0 + q, :] + dtb_ref[...])
        a_cs = _dot(tril_ref[...], jnp.concatenate(_split3(dt * a_neg), axis=0))
        ea = jnp.exp(a_cs)
        eds = jnp.exp(a_cs[q - 1:q, :] - a_cs)
        stacked = jnp.concatenate([ea, eds * dt], axis=0)
        hi = stacked.astype(BF16)
        mid = (stacked - hi.astype(F32)).astype(BF16)
        wide = _dot(jnp.concatenate([hi, mid], axis=1), e_ref[...])
        ea_x, w_x = wide[0:q], wide[q:2 * q]
        a_cs_t = a_cs.T
        dt_t = dt.T

        xs_b = xs.astype(BF16)
        xdt_s = (xs * w_x).astype(BF16)

        ys = []
        for g in range(SSM_GROUPS):
            lo = g * SSM_GROUP_W
            bg = act[:, D_MODEL + g * SSM_STATE:D_MODEL + (g + 1) * SSM_STATE].astype(BF16)
            cg = act[:, D_MODEL + SSM_GROUPS * SSM_STATE + g * SSM_STATE:
                     D_MODEL + SSM_GROUPS * SSM_STATE + (g + 1) * SSM_STATE].astype(BF16)
            cb = _dot_nt(cg, bg)
            state = st_scr[g]
            y_off = _dot(cg, state.astype(BF16)) * ea_x[:, lo:lo + SSM_GROUP_W]
            y_diag = []
            for pp in range(2):
                xp = xs_b[:, lo + pp * LANES:lo + (pp + 1) * LANES]
                zero = jnp.zeros_like(xp)
                x_blocks = jnp.concatenate(
                    [jnp.where(first, xp, zero), jnp.where(first, zero, xp)], axis=0)
                mats = []
                for hh in range(2):
                    hl = N_SMALL + g * 4 + pp * 2 + hh
                    seg = a_cs[:, hl:hl + 1] - a_cs_t[hl:hl + 1, :]
                    decay = jnp.where(tril, jnp.exp(seg), 0.0) * dt_t[hl:hl + 1, :]
                    mats.append((cb * decay).astype(BF16))
                y_diag.append(_dot(jnp.concatenate(mats, axis=1), x_blocks))
            new_states = lax.dot_general(bg, xdt_s[:, lo:lo + SSM_GROUP_W],
                                         (((0,), (0,)), ((), ())), preferred_element_type=F32)
            st_scr[g] = state * ea_x[q - 1:q, lo:lo + SSM_GROUP_W] + new_states
            ys.append(jnp.concatenate(y_diag, axis=1) + y_off)

        y = jnp.concatenate(ys, axis=1) + xs * dskip_ref[...]
        z = z_ref[r0:r0 + q, :].astype(F32)
        yg = y * _silu(z)
        outs = []
        for g in range(SSM_GROUPS):
            v = yg[:, g * SSM_GROUP_W:(g + 1) * SSM_GROUP_W]
            ms = jnp.mean(v * v, axis=-1, keepdims=True)
            outs.append(v * lax.rsqrt(ms + EPS))
        o_ref[r0:r0 + q, :] = (jnp.concatenate(outs, axis=1) * nw_ref[...]).astype(BF16)

    u_scr[CONV_PAD - CONV_HALO:CONV_PAD, :] = xbc_ref[rows - CONV_HALO:rows, :]


def _conv_shift_matrix(q):
    m = np.zeros((3 * q, CONV_PAD + q), np.float32)
    for k in range(CONV_K - 1):
        t = np.arange(q)
        m[k * q + t, CONV_PAD + t - (CONV_K - 1) + k] = 1.0
    return jnp.asarray(m, BF16)


def _ssd(p, small, conv_w, conv_b, dtb_pad, alog_pad, dskip_x, norm_w, expand, bsz, seq):
    q = min(SSD_CHUNK, seq)
    nsub = SSD_SUBCHUNKS if seq % (q * SSD_SUBCHUNKS) == 0 else 1
    rows = q * nsub
    nc = seq // rows
    const = lambda shape: pl.BlockSpec(shape, lambda b, c: (0, 0))
    return pl.pallas_call(
        functools.partial(_ssd_kernel, q=q, nsub=nsub),
        out_shape=jax.ShapeDtypeStruct((bsz * seq, D_MODEL), BF16),
        grid=(bsz, nc),
        in_specs=[
            pl.BlockSpec((rows, 2 * D_MODEL), lambda b, c: (b * nc + c, C_XBC // (2 * D_MODEL))),
            pl.BlockSpec((rows, D_MODEL), lambda b, c: (b * nc + c, C_SZ // D_MODEL)),
            pl.BlockSpec((rows, LANES), lambda b, c: (b * nc + c, 0)),
            const((CONV_K, 2 * D_MODEL)), const((1, 2 * D_MODEL)),
            const((1, LANES)), const((1, LANES)),
            const((1, D_MODEL)), const((1, D_MODEL)),
            const((2 * LANES, D_MODEL)), const((3 * q, CONV_PAD + q)), const((q, 3 * q)),
        ],
        out_specs=pl.BlockSpec((rows, D_MODEL), lambda b, c: (b * nc + c, 0)),
        scratch_shapes=[pltpu.VMEM((CONV_PAD + rows, 2 * D_MODEL), BF16),
                        pltpu.VMEM((SSM_GROUPS, SSM_STATE, SSM_GROUP_W), F32)],
        compiler_params=pltpu.CompilerParams(
            dimension_semantics=("arbitrary", "arbitrary"), vmem_limit_bytes=VMEM_LIMIT),
        name="ssd",
    )(p, p, small, conv_w, conv_b, dtb_pad, alog_pad, dskip_x, norm_w, expand,
      _conv_shift_matrix(q), jnp.asarray(np.tile(np.tril(np.ones((q, q), np.float32)), (1, 3)), BF16))


def _merge_kernel(x_ref, ya_ref, yb_ref, yc_ref, g0_ref, g1_ref, g2_ref, wb_ref, wo_ref,
                  fw_ref, o_ref, *, final):
    merged = None
    for n, (y_ref, g_ref) in enumerate(((ya_ref, g0_ref), (yb_ref, g1_ref), (yc_ref, g2_ref))):
        term = _sigmoid(g_ref[...].astype(F32)) * _dot(y_ref[...], wb_ref[n])
        merged = term if merged is None else merged + term
    out = x_ref[...] + _dot(merged.astype(BF16), wo_ref[...])
    if final:
        ms = jnp.mean(out * out, axis=-1, keepdims=True)
        out = out * lax.rsqrt(ms + EPS) * fw_ref[...]
    o_ref[...] = out


def _merge(x2, y_a, y_b, y_c, p, w_branch, w_out, final_w, final):
    t = x2.shape[0]
    tm = min(MERGE_TM, t)
    rows = lambda: pl.BlockSpec((tm, D_MODEL), lambda i: (i, 0))
    gate = lambda n: pl.BlockSpec((tm, D_MODEL), lambda i, n=n: (i, C_MG // D_MODEL + n))
    return pl.pallas_call(
        functools.partial(_merge_kernel, final=final),
        out_shape=jax.ShapeDtypeStruct((t, D_MODEL), F32),
        grid=(t // tm,),
        in_specs=[rows(), rows(), rows(), rows(), gate(0), gate(1), gate(2),
                  pl.BlockSpec((3, D_MODEL, D_MODEL), lambda i: (0, 0, 0)),
                  pl.BlockSpec((D_MODEL, D_MODEL), lambda i: (0, 0)),
                  pl.BlockSpec((1, D_MODEL), lambda i: (0, 0))],
        out_specs=rows(),
        compiler_params=pltpu.CompilerParams(
            dimension_semantics=("arbitrary",), vmem_limit_bytes=VMEM_LIMIT),
        name="merge",
    )(x2, y_a, y_b, y_c, p, p, p, w_branch, w_out, final_w)


def _rope_tables(seq):
    pos = jnp.arange(seq, dtype=F32)
    inv_freq = ROPE_THETA ** (-jnp.arange(0, 2 * ROT_HALF, 2, dtype=F32) / (2 * ROT_HALF))
    ang = pos[:, None] * inv_freq[None, :]
    cos, sin = jnp.cos(ang), jnp.sin(ang)
    ones = jnp.ones((seq, HEAD_DIM - 2 * ROT_HALF), F32)
    zeros = jnp.zeros((seq, ROT_HALF), F32)
    rest = jnp.zeros((seq, HEAD_DIM - 2 * ROT_HALF), F32)
    cos_c = jnp.concatenate([cos, cos, ones], axis=1)
    s1_c = jnp.concatenate([-sin, zeros, rest], axis=1)
    s2_c = jnp.concatenate([zeros, sin, rest], axis=1)
    two = lambda t: jnp.concatenate([t, t], axis=1)
    return two(cos_c), two(s1_c), two(s2_c)


def _pad_lanes(v, start):
    return jnp.zeros((1, LANES), F32).at[0, start:start + v.shape[0]].set(v.astype(F32))


_SEGMENTS = ((0, C_FQ, 1024, True), (1024, C_FK, 1024, False), (2048, C_FV, 1024, False),
             (3088, C_FG, 1024, False), (5136, C_XBC, 2048, False), (4112, C_SZ, 1024, False),
             (7200, C_DQ, 1024, True), (8224, C_DK, 1024, False), (9248, C_DV, 1024, False),
             (10272, C_DG, 1024, False), (11296, C_MG, 3072, False))
_SRC_FORGET, _SRC_DT, _N_IN = 3072, 7184, 14368


def _wprep_kernel(wt_ref, wm_ref, ws_ref):
    scale = LOG2E * HEAD_DIM ** -0.5
    for src, dst, n, scaled in _SEGMENTS:
        x = wt_ref[src:src + n, :]
        wm_ref[dst:dst + n, :] = (x * scale if scaled else x).astype(BF16)
    ws_ref[...] = jnp.concatenate(
        [wt_ref[_SRC_FORGET:_SRC_FORGET + N_SMALL, :], wt_ref[_SRC_DT:_SRC_DT + N_SMALL, :],
         jnp.zeros((LANES - 2 * N_SMALL, wt_ref.shape[1]), F32)], axis=0).astype(BF16)


def _projection_weights(w_in_t, layer):
    return pl.pallas_call(
        _wprep_kernel,
        out_shape=(jax.ShapeDtypeStruct((N_MAIN, D_MODEL), BF16),
                   jax.ShapeDtypeStruct((LANES, D_MODEL), BF16)),
        grid=(D_MODEL // LANES,),
        in_specs=[pl.BlockSpec((None, _N_IN, LANES), lambda i: (layer, 0, i))],
        out_specs=(pl.BlockSpec((N_MAIN, LANES), lambda i: (0, i)),
                   pl.BlockSpec((LANES, LANES), lambda i: (0, i))),
        compiler_params=pltpu.CompilerParams(
            dimension_semantics=("arbitrary",), vmem_limit_bytes=VMEM_LIMIT),
        name="wprep",
    )(w_in_t)


def kernel(x, norm_w, w_in, b_forget, conv_w, conv_b, dt_bias, a_log, d_skip, ssm_norm_w,
           diff_lambda, subln_w, w_branch, w_out, final_norm_w):
    bsz, seq, _ = x.shape
    depth = norm_w.shape[0]
    cos_t, s1_t, s2_t = _rope_tables(seq)
    heads = jnp.arange(D_MODEL) // HEAD_DIM
    expand = (jnp.arange(2 * LANES)[:, None] % LANES == (heads[None, :] + N_SMALL)).astype(BF16)

    w_in_t = jnp.swapaxes(w_in, 1, 2)
    w_branch_b, w_out_b = w_branch.astype(BF16), w_out.astype(BF16)

    x2 = x.reshape(bsz * seq, D_MODEL)
    for layer in range(depth):
        w_main_t, w_small_t = _projection_weights(w_in_t, layer)
        p, small = _inproj(x2, norm_w[layer][None, :], w_main_t, w_small_t)

        ft = _fprep(small, _pad_lanes(b_forget[layer], 0), bsz, seq)
        y_a = _fox(p, ft.reshape(bsz, N_SMALL // 2, 2, seq), bsz, seq)

        y_b = _ssd(p, small, conv_w[layer], conv_b[layer][None, :],
                   _pad_lanes(dt_bias[layer], N_SMALL),
                   _pad_lanes(a_log[layer], N_SMALL),
                   jnp.repeat(d_skip[layer], HEAD_DIM)[None, :],
                   ssm_norm_w[layer][None, :], expand, bsz, seq)

        lam_init = 0.8 - 0.6 * math.exp(-0.3 * layer)
        y_c = _diff(p, cos_t, s1_t, s2_t, diff_lambda[layer], subln_w[layer][None, :],
                    bsz, seq, lam_init)

        x2 = _merge(x2, y_a, y_b, y_c, p, w_branch_b[layer], w_out_b[layer],
                    final_norm_w[None, :], layer == depth - 1)
    return x2.reshape(bsz, seq, D_MODEL)
```

```python
import functools
import math

import numpy as np
import jax
import jax.numpy as jnp
from jax import lax
from jax.experimental import pallas as pl
from jax.experimental.pallas import tpu as pltpu

F32 = jnp.float32
BF16 = jnp.bfloat16

D_MODEL = 1024
EPS = 1e-6
LOG2E = 1.4426950408889634
HEAD_DIM = 64
LANES = 128
N_SMALL = 16
SSM_GROUPS = 4
SSM_STATE = 128
SSM_GROUP_W = 256
CONV_K = 4
CONV_HALO = 16
CONV_PAD = 128
ROT_HALF = 8
ROPE_THETA = 500000.0
MASK_CHUNK = 64
SUM_ROWS = 16

C_FQ, C_FK, C_FV, C_FG = 0, 1024, 2048, 3072
C_XBC, C_SZ = 4096, 6144
C_DQ, C_DK, C_DV, C_DG = 7168, 8192, 9216, 10240
C_MG = 11264
N_MAIN = 14336

ATT_TILE = 256
FOX_PAIRS = 2
DIFF_GROUP = 2
SSD_CHUNK = 128
SSD_SUBCHUNKS = 4
PROJ_TM, PROJ_TN = 2048, 1024
MERGE_TM = 512
VMEM_LIMIT = 56 * 1024 * 1024


def _sigmoid(x):
    return 0.5 * jnp.tanh(0.5 * x) + 0.5


def _silu_of_half(h):
    return h + h * jnp.tanh(h)


def _softplus(x):
    return jnp.maximum(x, 0.0) + jnp.log1p(jnp.exp(-jnp.abs(x)))


def _split3(x):
    hi = x.astype(BF16)
    r1 = x - hi.astype(F32)
    mid = r1.astype(BF16)
    lo = (r1 - mid.astype(F32)).astype(BF16)
    return hi, mid, lo


def _dot(a, b):
    return jnp.dot(a, b, preferred_element_type=F32)


def _dot_nt(a, b):
    return lax.dot_general(a, b, (((1,), (1,)), ((), ())), preferred_element_type=F32)


def _exact_dot_l(m01, x):
    hi, mid, lo = _split3(x)
    return _dot(m01, hi) + _dot(m01, mid) + _dot(m01, lo)


def _tril_bf16(n):
    r = lax.broadcasted_iota(jnp.int32, (n, n), 0)
    c = lax.broadcasted_iota(jnp.int32, (n, n), 1)
    return jnp.where(r >= c, 1.0, 0.0).astype(BF16)


def _inproj_kernel(x_ref, nw_ref, w_ref, ws_ref, p_ref, s_ref, h_ref):
    @pl.when(pl.program_id(1) == 0)
    def _():
        x = x_ref[...]
        ms = jnp.mean(x * x, axis=-1, keepdims=True)
        h = (x * lax.rsqrt(ms + EPS) * nw_ref[...]).astype(BF16)
        h_ref[...] = h
        s_ref[...] = _dot_nt(h, ws_ref[...])

    p_ref[...] = _dot_nt(h_ref[...], w_ref[...]).astype(BF16)


def _inproj(x2, norm_w, w_main_t, w_small_t):
    t = x2.shape[0]
    tm, tn = min(PROJ_TM, t), PROJ_TN
    return pl.pallas_call(
        _inproj_kernel,
        out_shape=(jax.ShapeDtypeStruct((t, N_MAIN), BF16),
                   jax.ShapeDtypeStruct((t, LANES), F32)),
        grid=(t // tm, N_MAIN // tn),
        in_specs=[
            pl.BlockSpec((tm, D_MODEL), lambda i, j: (i, 0)),
            pl.BlockSpec((1, D_MODEL), lambda i, j: (0, 0)),
            pl.BlockSpec((tn, D_MODEL), lambda i, j: (j, 0)),
            pl.BlockSpec((LANES, D_MODEL), lambda i, j: (0, 0)),
        ],
        out_specs=(pl.BlockSpec((tm, tn), lambda i, j: (i, j)),
                   pl.BlockSpec((tm, LANES), lambda i, j: (i, 0))),
        scratch_shapes=[pltpu.VMEM((tm, D_MODEL), BF16)],
        compiler_params=pltpu.CompilerParams(
            dimension_semantics=("arbitrary", "arbitrary"), vmem_limit_bytes=VMEM_LIMIT),
        name="inproj",
    )(x2, norm_w, w_main_t, w_small_t)


def _fprep_kernel(s_ref, bias_ref, ft_ref, f_scr, *, seq):
    tril = _tril_bf16(LANES)
    carry = jnp.zeros((1, LANES), F32)
    for blk in range(seq // LANES):
        x = s_ref[blk * LANES:(blk + 1) * LANES, :] + bias_ref[...]
        log_f = jnp.minimum(x, 0.0) - jnp.log1p(jnp.exp(-jnp.abs(x)))
        c = _exact_dot_l(tril, log_f) + carry
        f_scr[blk * LANES:(blk + 1) * LANES, :] = c
        carry = c[LANES - 1:LANES, :]
    ft_ref[...] = f_scr[...].T[0:N_SMALL, :]


def _fprep(small, bias_pad, bsz, seq):
    return pl.pallas_call(
        functools.partial(_fprep_kernel, seq=seq),
        out_shape=jax.ShapeDtypeStruct((bsz, N_SMALL, seq), F32),
        grid=(bsz,),
        in_specs=[pl.BlockSpec((seq, LANES), lambda b: (b, 0)),
                  pl.BlockSpec((1, LANES), lambda b: (0, 0))],
        out_specs=pl.BlockSpec((None, N_SMALL, seq), lambda b: (b, 0, 0)),
        scratch_shapes=[pltpu.VMEM((seq, LANES), F32)],
        compiler_params=pltpu.CompilerParams(
            dimension_semantics=("arbitrary",), vmem_limit_bytes=VMEM_LIMIT),
        name="fprep",
    )(small, bias_pad)


def _sublane_tiles(x, op):
    t = x[0:8, :]
    for r in range(1, x.shape[0] // 8):
        t = op(t, x[8 * r:8 * (r + 1), :])
    return t


def _scores(k_refs, qts, h, j, tile, mask):
    s = _dot(k_refs[h][j * tile:(j + 1) * tile, :], qts[h])
    return s if mask is None else jnp.where(mask, s, -jnp.inf)


def _two_pass_sweep(n_tiles, prepare, make_queries, k_refs, vt_refs, scr, tile, mask, finish,
                    early_exp):
    n = len(k_refs)
    bufs = lambda i: [scr[n * (i % 2) + h] for h in range(n)]

    def pass1(i):
        prepare(i)
        qts, ms = make_queries(i), [[] for _ in range(n)]
        for h in range(n):
            for j in range(i + 1):
                s = _scores(k_refs, qts, h, j, tile, mask if j == i else None)
                t = _sublane_tiles(s, jnp.maximum)
                if early_exp:
                    m = jnp.max(t, axis=0, keepdims=True)
                    m = jnp.maximum(ms[h][-1], m) if ms[h] else m
                    ms[h].append(m)
                    bufs(i)[h][j * tile:(j + 1) * tile, :] = jnp.exp2(s - m).astype(BF16)
                else:
                    ms[h] = [jnp.maximum(ms[h][0], t) if ms[h] else t]
                    bufs(i)[h][j * tile:(j + 1) * tile, :] = s
        if not early_exp:
            ms = [[jnp.max(t[0], axis=0, keepdims=True)] for t in ms]
        return ms

    def pass2(i, ms):
        accs = [None] * n
        for h in range(n):
            for j in range(i + 1):
                w = bufs(i)[h][j * tile:(j + 1) * tile, :]
                if not early_exp:
                    w = jnp.exp2(w - ms[h][0]).astype(BF16)
                pv = _dot(vt_refs[h][:, j * tile:(j + 1) * tile], w)
                if early_exp and j < i:
                    pv = pv * jnp.exp2(ms[h][j] - ms[h][i])
                accs[h] = pv if accs[h] is None else accs[h] + pv
        return accs

    ms = pass1(0)
    for i in range(n_tiles):
        ms_next = pass1(i + 1) if i + 1 < n_tiles else None
        finish(i, pass2(i, ms))
        ms = ms_next


def _fox_kernel(q_ref, k_ref, v_ref, g_ref, ft_ref, o_ref, k_scr, vt_scr, *s_scr, seq, tile):
    lane = lax.broadcasted_iota(jnp.int32, (1, LANES), 1)
    sub = lax.broadcasted_iota(jnp.int32, (LANES, 1), 0)
    key = lax.broadcasted_iota(jnp.int32, (tile, tile), 0)
    qry = lax.broadcasted_iota(jnp.int32, (tile, tile), 1)
    causal = key <= qry
    own = (lane < HEAD_DIM, lane >= HEAD_DIM)
    own_t = (sub < HEAD_DIM, sub >= HEAD_DIM)
    bias_lo = (HEAD_DIM, 0)
    ones_q = tuple(jnp.logical_and(lane >= b, lane < b + 3).astype(F32) for b in bias_lo)
    streams = [(pr, hh) for pr in range(FOX_PAIRS) for hh in range(2)]

    def prepare(i):
        r0 = i * tile
        for pr in range(FOX_PAIRS):
            k = k_ref[r0:r0 + tile, pr * LANES:(pr + 1) * LANES]
            vt = v_ref[r0:r0 + tile, pr * LANES:(pr + 1) * LANES].astype(F32).T
            for hh in range(2):
                b = bias_lo[hh]
                hi, mid, lo = _split3(ft_ref[pr, hh:hh + 1, r0:r0 + tile] * (-LOG2E))
                rows = jnp.concatenate([hi.astype(F32), mid.astype(F32), lo.astype(F32),
                                        jnp.zeros((5, tile), F32)], axis=0)
                block = jnp.concatenate(([jnp.zeros((b, tile), F32)] if b else []) + [rows]
                                        + [jnp.zeros((LANES - b - 8, tile), F32)], axis=0)
                st = 2 * pr + hh
                k_scr[st, r0:r0 + tile, :] = jnp.where(own[hh], k, block.T.astype(BF16))
                vt_scr[st, :, r0:r0 + tile] = jnp.where(own_t[hh], vt, 1.0).astype(BF16)

    def queries(i):
        out = []
        for pr, hh in streams:
            q = q_ref[i * tile:(i + 1) * tile, pr * LANES:(pr + 1) * LANES].astype(F32)
            out.append(jnp.where(own[hh], q, ones_q[hh]).T.astype(BF16))
        return out

    def finish(i, accs):
        r0 = i * tile
        for pr in range(FOX_PAIRS):
            a0, a1 = accs[2 * pr], accs[2 * pr + 1]
            num = jnp.where(own_t[0], a0, a1)
            den = jnp.where(own_t[0], a0[HEAD_DIM:HEAD_DIM + 1, :], a1[0:1, :])
            g = g_ref[r0:r0 + tile, pr * LANES:(pr + 1) * LANES].astype(F32)
            o_ref[r0:r0 + tile, pr * LANES:(pr + 1) * LANES] = (
                (num / den).T * _silu_of_half(g)).astype(BF16)

    n = len(streams)
    _two_pass_sweep(seq // tile, prepare, queries, [k_scr.at[st] for st in range(n)],
                    [vt_scr.at[st] for st in range(n)], s_scr, tile, causal, finish, True)


def _fox(p, ft4, bsz, seq):
    tile = min(ATT_TILE, seq)
    w = FOX_PAIRS * LANES
    blk = lambda off: pl.BlockSpec((seq, w), lambda b, h, off=off: (b, off // w + h))
    n = 2 * FOX_PAIRS
    return pl.pallas_call(
        functools.partial(_fox_kernel, seq=seq, tile=tile),
        out_shape=jax.ShapeDtypeStruct((bsz * seq, D_MODEL), BF16),
        grid=(bsz, D_MODEL // w),
        in_specs=[blk(C_FQ), blk(C_FK), blk(C_FV), blk(C_FG),
                  pl.BlockSpec((None, FOX_PAIRS, 2, seq), lambda b, h: (b, h, 0, 0))],
        out_specs=pl.BlockSpec((seq, w), lambda b, h: (b, h)),
        scratch_shapes=[pltpu.VMEM((n, seq, LANES), BF16), pltpu.VMEM((n, LANES, seq), BF16)]
        + [pltpu.VMEM((seq, tile), BF16)] * (2 * n),
        compiler_params=pltpu.CompilerParams(
            dimension_semantics=("arbitrary", "arbitrary"), vmem_limit_bytes=VMEM_LIMIT),
        name="fox",
    )(p, p, p, p, ft4)


def _diff_kernel(q_ref, k_ref, v_ref, g_ref, cos_ref, s1_ref, s2_ref, lam_ref, sub_ref,
                 o_ref, k_scr, vt_scr, *s_scr, seq, tile, lam_init):
    def rope(x, r0):
        return (x * cos_ref[r0:r0 + tile, :]
                + pltpu.roll(x, LANES - ROT_HALF, 1) * s1_ref[r0:r0 + tile, :]
                + pltpu.roll(x, ROT_HALF, 1) * s2_ref[r0:r0 + tile, :])

    def prepare(i):
        r0 = i * tile
        for hd in range(DIFF_GROUP):
            cols = slice(hd * LANES, (hd + 1) * LANES)
            k_scr[hd, r0:r0 + tile, :] = rope(k_ref[r0:r0 + tile, cols].astype(F32), r0).astype(BF16)
            vt = v_ref[r0:r0 + tile, cols].astype(F32).T
            vt_scr[hd, :, r0:r0 + tile] = jnp.concatenate(
                [vt, jnp.ones((SUM_ROWS, tile), F32)], axis=0).astype(BF16)

    lp = lam_ref[...]
    lam = (jnp.exp(jnp.sum(lp[0:1] * lp[1:2], axis=1, keepdims=True))
           - jnp.exp(jnp.sum(lp[2:3] * lp[3:4], axis=1, keepdims=True)) + lam_init)

    lane = lax.broadcasted_iota(jnp.int32, (1, LANES), 1)
    first = lane < HEAD_DIM
    key = lax.broadcasted_iota(jnp.int32, (tile, tile), 0)
    qry = lax.broadcasted_iota(jnp.int32, (tile, tile), 1)
    chunk_causal = (key // MASK_CHUNK) <= (qry // MASK_CHUNK)

    def queries(i):
        r0 = i * tile
        out = []
        for hd in range(DIFF_GROUP):
            q = rope(q_ref[r0:r0 + tile, hd * LANES:(hd + 1) * LANES].astype(F32), r0)
            out += [jnp.where(first, q, 0.0).T.astype(BF16), jnp.where(first, 0.0, q).T.astype(BF16)]
        return out

    def finish(i, accs):
        r0 = i * tile
        for hd in range(DIFF_GROUP):
            a0, a1 = accs[2 * hd], accs[2 * hd + 1]
            o = (a0[0:LANES] / a0[LANES:LANES + 1] - lam * (a1[0:LANES] / a1[LANES:LANES + 1])).T
            ms = jnp.mean(o * o, axis=-1, keepdims=True)
            o = o * lax.rsqrt(ms + EPS) * sub_ref[...] * (1.0 - lam_init)
            g = g_ref[r0:r0 + tile, hd * LANES:(hd + 1) * LANES].astype(F32)
            o_ref[r0:r0 + tile, hd * LANES:(hd + 1) * LANES] = (o * _silu_of_half(g)).astype(BF16)

    heads = [hd for hd in range(DIFF_GROUP) for _ in range(2)]
    _two_pass_sweep(seq // tile, prepare, queries, [k_scr.at[hd] for hd in heads],
                    [vt_scr.at[hd] for hd in heads], s_scr, tile, chunk_causal, finish, False)


def _diff(p, cos_t, s1_t, s2_t, lam_p, sub_w, bsz, seq, lam_init):
    tile = min(ATT_TILE, seq)
    w = DIFF_GROUP * LANES
    blk = lambda off: pl.BlockSpec((seq, w), lambda b, h, off=off: (b, off // w + h))
    const = lambda shape: pl.BlockSpec(shape, lambda b, h: (0, 0))
    return pl.pallas_call(
        functools.partial(_diff_kernel, seq=seq, tile=tile, lam_init=lam_init),
        out_shape=jax.ShapeDtypeStruct((bsz * seq, D_MODEL), BF16),
        grid=(bsz, D_MODEL // w),
        in_specs=[blk(C_DQ), blk(C_DK), blk(C_DV), blk(C_DG),
                  const((seq, LANES)), const((seq, LANES)), const((seq, LANES)),
                  const((4, HEAD_DIM)), const((1, LANES))],
        out_specs=pl.BlockSpec((seq, w), lambda b, h: (b, h)),
        scratch_shapes=[pltpu.VMEM((DIFF_GROUP, seq, LANES), BF16),
                        pltpu.VMEM((DIFF_GROUP, LANES + SUM_ROWS, seq), BF16)]
        + [pltpu.VMEM((seq, tile), F32)] * (4 * DIFF_GROUP),
        compiler_params=pltpu.CompilerParams(
            dimension_semantics=("arbitrary", "arbitrary"), vmem_limit_bytes=VMEM_LIMIT),
        name="diffattn",
    )(p, p, p, p, cos_t, s1_t, s2_t, lam_p, sub_w)


def _ssd_kernel(xbc_ref, z_ref, sm_ref, cw_ref, cb_ref, dtb_ref, alog_ref, dskip_ref, nw_ref,
                e_ref, shift_ref, tril_ref, o_ref, u_scr, st_scr, *, q, nsub):
    @pl.when(pl.program_id(1) == 0)
    def _():
        u_scr[0:CONV_PAD, :] = jnp.zeros((CONV_PAD, 2 * D_MODEL), BF16)
        st_scr[...] = jnp.zeros_like(st_scr)

    rows = q * nsub
    u_scr[CONV_PAD:CONV_PAD + rows, :] = xbc_ref[...]
    lane = lax.broadcasted_iota(jnp.int32, (1, LANES), 1)
    is_head = jnp.logical_and(lane >= N_SMALL, lane < 2 * N_SMALL)
    a_neg = jnp.where(is_head, -jnp.exp(alog_ref[...]), 0.0)
    row = lax.broadcasted_iota(jnp.int32, (q, q), 0)
    col = lax.broadcasted_iota(jnp.int32, (q, q), 1)
    tril = col <= row
    first = lane < HEAD_DIM

    for sc in range(nsub):
        r0 = sc * q
        delayed = _dot(shift_ref[...], u_scr[r0:r0 + CONV_PAD + q, :])
        conv = cb_ref[...] + cw_ref[CONV_K - 1:CONV_K, :] * xbc_ref[r0:r0 + q, :].astype(F32)
        for kk in range(CONV_K - 1):
            conv = conv + cw_ref[kk:kk + 1, :] * delayed[kk * q:(kk + 1) * q, :]
        act = _silu_of_half(conv)
        xs = act[:, 0:D_MODEL]

        dt = _softplus(sm_ref[r0:r0 + q, :] + dtb_ref[...])
        a_cs = _dot(tril_ref[...], jnp.concatenate(_split3(dt * a_neg), axis=0))
        ea = jnp.exp(a_cs)
        eds = jnp.exp(a_cs[q - 1:q, :] - a_cs)
        stacked = jnp.concatenate([ea, eds * dt], axis=0)
        hi = stacked.astype(BF16)
        mid = (stacked - hi.astype(F32)).astype(BF16)
        wide = _dot(jnp.concatenate([hi, mid], axis=1), e_ref[...])
        ea_x, w_x = wide[0:q], wide[q:2 * q]
        a_cs_t = a_cs.T
        dt_t = dt.T

        xs_b = xs.astype(BF16)
        xdt_s = (xs * w_x).astype(BF16)

        ys = []
        for g in range(SSM_GROUPS):
            lo = g * SSM_GROUP_W
            bg = act[:, D_MODEL + g * SSM_STATE:D_MODEL + (g + 1) * SSM_STATE].astype(BF16)
            cg = act[:, D_MODEL + SSM_GROUPS * SSM_STATE + g * SSM_STATE:
                     D_MODEL + SSM_GROUPS * SSM_STATE + (g + 1) * SSM_STATE].astype(BF16)
            cb = _dot_nt(cg, bg)
            state = st_scr[g]
            y_off = _dot(cg, state.astype(BF16)) * ea_x[:, lo:lo + SSM_GROUP_W]
            y_diag = []
            for pp in range(2):
                xp = xs_b[:, lo + pp * LANES:lo + (pp + 1) * LANES]
                zero = jnp.zeros_like(xp)
                x_blocks = jnp.concatenate(
                    [jnp.where(first, xp, zero), jnp.where(first, zero, xp)], axis=0)
                mats = []
                for hh in range(2):
                    hl = N_SMALL + g * 4 + pp * 2 + hh
                    seg = a_cs[:, hl:hl + 1] - a_cs_t[hl:hl + 1, :]
                    decay = jnp.where(tril, jnp.exp(seg), 0.0) * dt_t[hl:hl + 1, :]
                    mats.append((cb * decay).astype(BF16))
                y_diag.append(_dot(jnp.concatenate(mats, axis=1), x_blocks))
            new_states = lax.dot_general(bg, xdt_s[:, lo:lo + SSM_GROUP_W],
                                         (((0,), (0,)), ((), ())), preferred_element_type=F32)
            st_scr[g] = state * ea_x[q - 1:q, lo:lo + SSM_GROUP_W] + new_states
            ys.append(jnp.concatenate(y_diag, axis=1) + y_off)

        y = jnp.concatenate(ys, axis=1) + xs * dskip_ref[...]
        z = z_ref[r0:r0 + q, :].astype(F32)
        yg = y * _silu_of_half(z)
        outs = []
        for g in range(SSM_GROUPS):
            v = yg[:, g * SSM_GROUP_W:(g + 1) * SSM_GROUP_W]
            ms = jnp.mean(v * v, axis=-1, keepdims=True)
            outs.append(v * lax.rsqrt(ms + EPS))
        o_ref[r0:r0 + q, :] = (jnp.concatenate(outs, axis=1) * nw_ref[...]).astype(BF16)

    u_scr[CONV_PAD - CONV_HALO:CONV_PAD, :] = xbc_ref[rows - CONV_HALO:rows, :]


def _conv_shift_matrix(q):
    m = np.zeros((3 * q, CONV_PAD + q), np.float32)
    for k in range(CONV_K - 1):
        t = np.arange(q)
        m[k * q + t, CONV_PAD + t - (CONV_K - 1) + k] = 1.0
    return jnp.asarray(m, BF16)


def _ssd(p, small, conv_w, conv_b, dtb_pad, alog_pad, dskip_x, norm_w, expand, bsz, seq):
    q = min(SSD_CHUNK, seq)
    nsub = SSD_SUBCHUNKS if seq % (q * SSD_SUBCHUNKS) == 0 else 1
    rows = q * nsub
    nc = seq // rows
    const = lambda shape: pl.BlockSpec(shape, lambda b, c: (0, 0))
    return pl.pallas_call(
        functools.partial(_ssd_kernel, q=q, nsub=nsub),
        out_shape=jax.ShapeDtypeStruct((bsz * seq, D_MODEL), BF16),
        grid=(bsz, nc),
        in_specs=[
            pl.BlockSpec((rows, 2 * D_MODEL), lambda b, c: (b * nc + c, C_XBC // (2 * D_MODEL))),
            pl.BlockSpec((rows, D_MODEL), lambda b, c: (b * nc + c, C_SZ // D_MODEL)),
            pl.BlockSpec((rows, LANES), lambda b, c: (b * nc + c, 0)),
            const((CONV_K, 2 * D_MODEL)), const((1, 2 * D_MODEL)),
            const((1, LANES)), const((1, LANES)),
            const((1, D_MODEL)), const((1, D_MODEL)),
            const((2 * LANES, D_MODEL)), const((3 * q, CONV_PAD + q)), const((q, 3 * q)),
        ],
        out_specs=pl.BlockSpec((rows, D_MODEL), lambda b, c: (b * nc + c, 0)),
        scratch_shapes=[pltpu.VMEM((CONV_PAD + rows, 2 * D_MODEL), BF16),
                        pltpu.VMEM((SSM_GROUPS, SSM_STATE, SSM_GROUP_W), F32)],
        compiler_params=pltpu.CompilerParams(
            dimension_semantics=("arbitrary", "arbitrary"), vmem_limit_bytes=VMEM_LIMIT),
        name="ssd",
    )(p, p, small, conv_w, conv_b, dtb_pad, alog_pad, dskip_x, norm_w, expand,
      _conv_shift_matrix(q), jnp.asarray(np.tile(np.tril(np.ones((q, q), np.float32)), (1, 3)), BF16))


def _merge_kernel(x_ref, ya_ref, yb_ref, yc_ref, g0_ref, g1_ref, g2_ref, wb_ref, wo_ref,
                  fw_ref, o_ref, *, final):
    merged = None
    for n, (y_ref, g_ref) in enumerate(((ya_ref, g0_ref), (yb_ref, g1_ref), (yc_ref, g2_ref))):
        term = _sigmoid(g_ref[...].astype(F32)) * _dot(y_ref[...], wb_ref[n])
        merged = term if merged is None else merged + term
    out = x_ref[...] + _dot(merged.astype(BF16), wo_ref[...])
    if final:
        ms = jnp.mean(out * out, axis=-1, keepdims=True)
        out = out * lax.rsqrt(ms + EPS) * fw_ref[...]
    o_ref[...] = out


def _merge(x2, y_a, y_b, y_c, p, w_branch, w_out, final_w, final):
    t = x2.shape[0]
    tm = min(MERGE_TM, t)
    rows = lambda: pl.BlockSpec((tm, D_MODEL), lambda i: (i, 0))
    gate = lambda n: pl.BlockSpec((tm, D_MODEL), lambda i, n=n: (i, C_MG // D_MODEL + n))
    return pl.pallas_call(
        functools.partial(_merge_kernel, final=final),
        out_shape=jax.ShapeDtypeStruct((t, D_MODEL), F32),
        grid=(t // tm,),
        in_specs=[rows(), rows(), rows(), rows(), gate(0), gate(1), gate(2),
                  pl.BlockSpec((3, D_MODEL, D_MODEL), lambda i: (0, 0, 0)),
                  pl.BlockSpec((D_MODEL, D_MODEL), lambda i: (0, 0)),
                  pl.BlockSpec((1, D_MODEL), lambda i: (0, 0))],
        out_specs=rows(),
        compiler_params=pltpu.CompilerParams(
            dimension_semantics=("arbitrary",), vmem_limit_bytes=VMEM_LIMIT),
        name="merge",
    )(x2, y_a, y_b, y_c, p, p, p, w_branch, w_out, final_w)


def _rope_tables(seq):
    pos = jnp.arange(seq, dtype=F32)
    inv_freq = ROPE_THETA ** (-jnp.arange(0, 2 * ROT_HALF, 2, dtype=F32) / (2 * ROT_HALF))
    ang = pos[:, None] * inv_freq[None, :]
    cos, sin = jnp.cos(ang), jnp.sin(ang)
    ones = jnp.ones((seq, HEAD_DIM - 2 * ROT_HALF), F32)
    zeros = jnp.zeros((seq, ROT_HALF), F32)
    rest = jnp.zeros((seq, HEAD_DIM - 2 * ROT_HALF), F32)
    cos_c = jnp.concatenate([cos, cos, ones], axis=1)
    s1_c = jnp.concatenate([-sin, zeros, rest], axis=1)
    s2_c = jnp.concatenate([zeros, sin, rest], axis=1)
    two = lambda t: jnp.concatenate([t, t], axis=1)
    return two(cos_c), two(s1_c), two(s2_c)


def _pad_lanes(v, start):
    return jnp.zeros((1, LANES), F32).at[0, start:start + v.shape[0]].set(v.astype(F32))


_Q_SCALE = LOG2E * HEAD_DIM ** -0.5
_SEGMENTS = ((0, C_FQ, 1024, _Q_SCALE), (1024, C_FK, 1024, 1.0), (2048, C_FV, 1024, 1.0),
             (3088, C_FG, 1024, 0.5), (5136, C_XBC, 2048, 1.0), (4112, C_SZ, 1024, 0.5),
             (7200, C_DQ, 1024, _Q_SCALE), (8224, C_DK, 1024, 1.0), (9248, C_DV, 1024, 1.0),
             (10272, C_DG, 1024, 0.5), (11296, C_MG, 3072, 1.0))
_SRC_FORGET, _SRC_DT, _N_IN = 3072, 7184, 14368


def _wprep_kernel(wt_ref, wm_ref, ws_ref):
    for src, dst, n, scale in _SEGMENTS:
        x = wt_ref[src:src + n, :]
        wm_ref[dst:dst + n, :] = (x if scale == 1.0 else x * scale).astype(BF16)
    ws_ref[...] = jnp.concatenate(
        [wt_ref[_SRC_FORGET:_SRC_FORGET + N_SMALL, :], wt_ref[_SRC_DT:_SRC_DT + N_SMALL, :],
         jnp.zeros((LANES - 2 * N_SMALL, wt_ref.shape[1]), F32)], axis=0).astype(BF16)


def _projection_weights(w_in_t, layer):
    return pl.pallas_call(
        _wprep_kernel,
        out_shape=(jax.ShapeDtypeStruct((N_MAIN, D_MODEL), BF16),
                   jax.ShapeDtypeStruct((LANES, D_MODEL), BF16)),
        grid=(D_MODEL // LANES,),
        in_specs=[pl.BlockSpec((None, _N_IN, LANES), lambda i: (layer, 0, i))],
        out_specs=(pl.BlockSpec((N_MAIN, LANES), lambda i: (0, i)),
                   pl.BlockSpec((LANES, LANES), lambda i: (0, i))),
        compiler_params=pltpu.CompilerParams(
            dimension_semantics=("arbitrary",), vmem_limit_bytes=VMEM_LIMIT),
        name="wprep",
    )(w_in_t)


def kernel(x, norm_w, w_in, b_forget, conv_w, conv_b, dt_bias, a_log, d_skip, ssm_norm_w,
           diff_lambda, subln_w, w_branch, w_out, final_norm_w):
    bsz, seq, _ = x.shape
    depth = norm_w.shape[0]
    cos_t, s1_t, s2_t = _rope_tables(seq)
    heads = jnp.arange(D_MODEL) // HEAD_DIM
    expand = (jnp.arange(2 * LANES)[:, None] % LANES == (heads[None, :] + N_SMALL)).astype(BF16)

    w_in_t = jnp.swapaxes(w_in, 1, 2)
    w_branch_b, w_out_b = w_branch.astype(BF16), w_out.astype(BF16)

    x2 = x.reshape(bsz * seq, D_MODEL)
    for layer in range(depth):
        w_main_t, w_small_t = _projection_weights(w_in_t, layer)
        p, small = _inproj(x2, norm_w[layer][None, :], w_main_t, w_small_t)

        ft = _fprep(small, _pad_lanes(b_forget[layer], 0), bsz, seq)
        y_a = _fox(p, ft.reshape(bsz, N_SMALL // 2, 2, seq), bsz, seq)

        y_b = _ssd(p, small, 0.5 * conv_w[layer], 0.5 * conv_b[layer][None, :],
                   _pad_lanes(dt_bias[layer], N_SMALL),
                   _pad_lanes(a_log[layer], N_SMALL),
                   jnp.repeat(d_skip[layer], HEAD_DIM)[None, :],
                   ssm_norm_w[layer][None, :], expand, bsz, seq)

        lam_init = 0.8 - 0.6 * math.exp(-0.3 * layer)
        y_c = _diff(p, cos_t, s1_t, s2_t, diff_lambda[layer], subln_w[layer][None, :],
                    bsz, seq, lam_init)

        x2 = _merge(x2, y_a, y_b, y_c, p, w_branch_b[layer], w_out_b[layer],
                    final_norm_w[None, :], layer == depth - 1)
    return x2.reshape(bsz, seq, D_MODEL)
```
